```python
import math
import jax
import jax.numpy as jnp
from jax import lax
import numpy as np

D_MODEL = 2048
BATCH = 8
SEQ = 2048
DEPTH = 1
DEC_BATCH = 32
DEC_SEQ = 8
PAST_LEN = 8192
PAGE_SIZE = 128

HEAD_DIM = 64
FOX_HEADS = 16
FOX_WIDTH = FOX_HEADS * HEAD_DIM
DIFF_HEADS = 8
DIFF_QK_WIDTH = 2 * DIFF_HEADS * HEAD_DIM
DIFF_V_DIM = 2 * HEAD_DIM
DIFF_V_WIDTH = DIFF_HEADS * DIFF_V_DIM
BRANCH_WIDTH = FOX_WIDTH
N_BRANCHES = 2
D_FF = 4 * D_MODEL
CONV_WIDTH = 3
ROPE_THETA = 10000.0
Q_BLOCK = 128
RMS_EPS = 1e-6
NEG_INF = -1e30
IN_SIZES = (FOX_WIDTH, FOX_WIDTH, FOX_WIDTH, FOX_HEADS,
            DIFF_QK_WIDTH, DIFF_QK_WIDTH, DIFF_V_WIDTH, N_BRANCHES * D_MODEL)
IN_WIDTH = sum(IN_SIZES)
IN_SPLITS = tuple(int(s) for s in np.cumsum(IN_SIZES)[:-1])

kernel_name = "fox_diffattn_gated_hybrid_decode_step"


def _lambda_init(layer_idx):
    return 0.8 - 0.6 * math.exp(-0.3 * layer_idx)


def rmsnorm(x, g):
    xf = x.astype(jnp.float32)
    y = xf * lax.rsqrt(jnp.mean(xf * xf, axis=-1, keepdims=True) + RMS_EPS)
    return (y * g.astype(jnp.float32)).astype(x.dtype)


def rope(x, pos):
    half = HEAD_DIM // 2
    inv_freq = ROPE_THETA ** (-jnp.arange(half, dtype=jnp.float32) / half)
    ang = pos.astype(jnp.float32)[:, None] * inv_freq[None, :]
    cos = jnp.cos(ang)[None, :, None, :]
    sin = jnp.sin(ang)[None, :, None, :]
    x1 = x[..., :half].astype(jnp.float32)
    x2 = x[..., half:].astype(jnp.float32)
    return jnp.concatenate([x1 * cos - x2 * sin, x2 * cos + x1 * sin], axis=-1).astype(x.dtype)


def _sweep(attend, n_q):
    if n_q <= Q_BLOCK or n_q % Q_BLOCK:
        return attend(0, n_q)
    out = lax.map(lambda i: attend(i * Q_BLOCK, Q_BLOCK), jnp.arange(n_q // Q_BLOCK))
    out = jnp.moveaxis(out, 0, 1)
    return out.reshape(out.shape[:1] + (n_q,) + out.shape[3:])


def _segmented_scores(qb, keys):
    return jnp.concatenate(
        [jnp.einsum('bqhd,bkhd->bhqk', qb, k, preferred_element_type=jnp.float32) for k in keys],
        axis=-1)


def _segmented_values(p, values):
    out, offset = None, 0
    for v in values:
        n = v.shape[1]
        o = jnp.einsum('bhqk,bkhe->bqhe', p[..., offset:offset + n].astype(v.dtype), v)
        out = o if out is None else out + o
        offset += n
    return out


def fox_attention(q, keys, values, c_q, c_k, q_pos, k_pos):
    scale = HEAD_DIM ** -0.5
    c_k_t = jnp.transpose(c_k, (0, 2, 1))

    def attend(start, size):
        qb = lax.dynamic_slice_in_dim(q, start, size, axis=1)
        cb = jnp.transpose(lax.dynamic_slice_in_dim(c_q, start, size, axis=1), (0, 2, 1))
        pb = lax.dynamic_slice_in_dim(q_pos, start, size, axis=0)
        s = _segmented_scores(qb, keys) * scale + (cb[..., :, None] - c_k_t[..., None, :])
        s = jnp.where(k_pos[None, :] <= pb[:, None], s, NEG_INF)
        return _segmented_values(jax.nn.softmax(s, axis=-1), values)

    return _sweep(attend, q.shape[1])


def diff_attention(q, keys, values, lam, q_pos, k_pos):
    scale = HEAD_DIM ** -0.5

    def attend(start, size):
        qb = lax.dynamic_slice_in_dim(q, start, size, axis=1)
        pb = lax.dynamic_slice_in_dim(q_pos, start, size, axis=0)
        s = _segmented_scores(qb, keys) * scale
        s = jnp.where(k_pos[None, :] <= pb[:, None], s, NEG_INF)
        p = jax.nn.softmax(s, axis=-1)
        p = p.reshape(p.shape[0], DIFF_HEADS, 2, size, p.shape[-1])
        a = p[:, :, 0] - lam * p[:, :, 1]
        return _segmented_values(a, values)

    return _sweep(attend, q.shape[1])


def token_mixers(h, n_past, past, w_in, b_forget, lam_q1, lam_k1, lam_q2, lam_k2,
                 diff_subln, w_branch, w_out, lambda_init):
    B, N, _ = h.shape
    proj = jnp.einsum('bnd,de->bne', h, w_in)
    fq, fk, fv, ff, dq, dk, dv, gates = jnp.split(proj, IN_SPLITS, axis=-1)
    q_pos = n_past + jnp.arange(N)
    k_pos = jnp.arange(n_past + N)

    fq = fq.reshape(B, N, FOX_HEADS, HEAD_DIM)
    fk = fk.reshape(B, N, FOX_HEADS, HEAD_DIM)
    fv = fv.reshape(B, N, FOX_HEADS, HEAD_DIM)
    logf = jax.nn.log_sigmoid(ff.astype(jnp.float32) + b_forget.astype(jnp.float32))

    dq = rope(dq.reshape(B, N, 2 * DIFF_HEADS, HEAD_DIM), q_pos)
    dk = rope(dk.reshape(B, N, 2 * DIFF_HEADS, HEAD_DIM), q_pos)
    dv = dv.reshape(B, N, DIFF_HEADS, DIFF_V_DIM)

    if past is None:
        fox_keys, fox_vals, logf_all = (fk,), (fv,), logf
        diff_keys, diff_vals = (dk,), (dv,)
    else:
        p_fk, p_fv, p_logf, p_dk, p_dv = past
        fox_keys, fox_vals = (p_fk, fk), (p_fv, fv)
        logf_all = jnp.concatenate([p_logf.astype(jnp.float32), logf], axis=1)
        diff_keys, diff_vals = (p_dk, dk), (p_dv, dv)

    c_all = jnp.cumsum(logf_all, axis=1)
    c_q = c_all[:, n_past:]
    fox_o = fox_attention(fq, fox_keys, fox_vals, c_q, c_all, q_pos, k_pos)

    lam = (jnp.exp(jnp.sum(lam_q1.astype(jnp.float32) * lam_k1.astype(jnp.float32)))
           - jnp.exp(jnp.sum(lam_q2.astype(jnp.float32) * lam_k2.astype(jnp.float32)))
           + lambda_init)
    diff_o = diff_attention(dq, diff_keys, diff_vals, lam, q_pos, k_pos)
    diff_o = rmsnorm(diff_o, diff_subln) * (1.0 - lambda_init)

    branches = jnp.stack([fox_o.reshape(B, N, FOX_WIDTH),
                          diff_o.reshape(B, N, DIFF_V_WIDTH)], axis=2)
    proj_br = jnp.einsum('bnrw,rwd->bnrd', branches, w_branch)
    g = jax.nn.sigmoid(gates.astype(jnp.float32)).astype(h.dtype).reshape(B, N, N_BRANCHES, D_MODEL)
    merged = jnp.sum(g * proj_br, axis=2)
    out = jnp.einsum('bnd,de->bne', merged, w_out)
    return out, (fk, fv, logf, dk, dv)


def conv_ffn(h, conv_state, w_up, conv_w, w_down):
    B, N, _ = h.shape
    u = jnp.einsum('bnd,df->bnf', h, w_up)
    if conv_state is None:
        prev = jnp.zeros((B, CONV_WIDTH - 1, 2 * D_FF), u.dtype)
    else:
        prev = conv_state.astype(u.dtype)
    u_ext = jnp.concatenate([prev, u], axis=1)
    y = conv_w[0] * u_ext[:, 0:N]
    for j in range(1, CONV_WIDTH):
        y = y + conv_w[j] * u_ext[:, j:j + N]
    gate, val = jnp.split(y, 2, axis=-1)
    act = jax.nn.gelu(gate, approximate=True) * val
    return jnp.einsum('bnf,fd->bnd', act, w_down), u_ext[:, -(CONV_WIDTH - 1):]


def _layer(x, n_past, past, conv_state, norm_mix_pre, norm_mix_post, w_in, b_forget,
           lam_q1, lam_k1, lam_q2, lam_k2, diff_subln, w_branch, w_out,
           norm_ffn_pre, norm_ffn_post, w_up, conv_w, w_down, lambda_init):
    mix, rows = token_mixers(rmsnorm(x, norm_mix_pre), n_past, past, w_in, b_forget,
                             lam_q1, lam_k1, lam_q2, lam_k2, diff_subln, w_branch, w_out,
                             lambda_init)
    x = x + rmsnorm(mix, norm_mix_post)
    f, new_conv = conv_ffn(rmsnorm(x, norm_ffn_pre), conv_state, w_up, conv_w, w_down)
    x = x + rmsnorm(f, norm_ffn_post)
    return x, rows, new_conv


def setup_inputs(seed: int = 0) -> dict:
    key = jax.random.key(seed)
    ks = jax.random.split(key, 28)
    f32 = jnp.float32
    n_pages = PAST_LEN // PAGE_SIZE
    n_used = DEC_BATCH * n_pages
    n_pool = n_used + n_used // 4

    def nrm(k, shape, scale=1.0):
        return jax.random.normal(k, shape, f32) * scale

    def gain(k, shape):
        return 1.0 + 0.05 * jax.random.normal(k, shape, f32)

    x_prompt = nrm(ks[0], (BATCH, SEQ, D_MODEL))
    x_sample = nrm(ks[1], (DEC_BATCH, DEC_SEQ, D_MODEL))
    cache_fox_k = nrm(ks[2], (DEPTH, n_pool, PAGE_SIZE, FOX_HEADS, HEAD_DIM))
    cache_fox_v = nrm(ks[3], (DEPTH, n_pool, PAGE_SIZE, FOX_HEADS, HEAD_DIM))
    cache_fox_logf = jax.nn.log_sigmoid(3.0 + nrm(ks[4], (DEPTH, n_pool, PAGE_SIZE, FOX_HEADS)))
    cache_diff_k = nrm(ks[5], (DEPTH, n_pool, PAGE_SIZE, 2 * DIFF_HEADS, HEAD_DIM))
    cache_diff_v = nrm(ks[6], (DEPTH, n_pool, PAGE_SIZE, DIFF_HEADS, DIFF_V_DIM))
    state_ffn_conv = nrm(ks[7], (DEPTH, DEC_BATCH, CONV_WIDTH - 1, 2 * D_FF))
    page_table = jax.random.permutation(ks[8], n_pool)[:n_used].reshape(DEC_BATCH, n_pages).astype(jnp.int32)

    return {
        "x_prompt": x_prompt,
        "x_sample": x_sample,
        "cache_fox_k": cache_fox_k,
        "cache_fox_v": cache_fox_v,
        "cache_fox_logf": cache_fox_logf,
        "cache_diff_k": cache_diff_k,
        "cache_diff_v": cache_diff_v,
        "state_ffn_conv": state_ffn_conv,
        "page_table": page_table,
        "norm_mix_pre": gain(ks[9], (DEPTH, D_MODEL)),
        "norm_mix_post": gain(ks[10], (DEPTH, D_MODEL)),
        "w_in": nrm(ks[11], (DEPTH, D_MODEL, IN_WIDTH), D_MODEL ** -0.5),
        "b_forget": jax.random.uniform(ks[12], (DEPTH, FOX_HEADS), f32, 1.0, 5.0),
        "lam_q1": nrm(ks[13], (DEPTH, HEAD_DIM), 0.1),
        "lam_k1": nrm(ks[14], (DEPTH, HEAD_DIM), 0.1),
        "lam_q2": nrm(ks[15], (DEPTH, HEAD_DIM), 0.1),
        "lam_k2": nrm(ks[16], (DEPTH, HEAD_DIM), 0.1),
        "diff_subln": gain(ks[17], (DEPTH, DIFF_V_DIM)),
        "w_branch": nrm(ks[18], (DEPTH, N_BRANCHES, BRANCH_WIDTH, D_MODEL), BRANCH_WIDTH ** -0.5),
        "w_out": nrm(ks[19], (DEPTH, D_MODEL, D_MODEL), D_MODEL ** -0.5),
        "norm_ffn_pre": gain(ks[20], (DEPTH, D_MODEL)),
        "norm_ffn_post": gain(ks[21], (DEPTH, D_MODEL)),
        "w_up": nrm(ks[22], (DEPTH, D_MODEL, 2 * D_FF), D_MODEL ** -0.5),
        "conv_ffn": nrm(ks[23], (DEPTH, CONV_WIDTH, 2 * D_FF), CONV_WIDTH ** -0.5),
        "w_down": nrm(ks[24], (DEPTH, D_FF, D_MODEL), D_FF ** -0.5),
    }


def reference(x_prompt, x_sample, cache_fox_k, cache_fox_v, cache_fox_logf, cache_diff_k,
              cache_diff_v, state_ffn_conv, page_table, norm_mix_pre, norm_mix_post, w_in,
              b_forget, lam_q1, lam_k1, lam_q2, lam_k2, diff_subln, w_branch, w_out,
              norm_ffn_pre, norm_ffn_post, w_up, conv_ffn, w_down):
    nb = page_table.shape[0]
    n_past = page_table.shape[1] * cache_fox_k.shape[2]
    xp, xs = x_prompt, x_sample
    p_fk, p_fv, p_lf, p_dk, p_dv, p_cv = [], [], [], [], [], []
    s_fk, s_fv, s_lf, s_dk, s_dv, s_cv = [], [], [], [], [], []
    for l in range(DEPTH):
        lam0 = _lambda_init(l)
        xp, rows_p, conv_p = _layer(
            xp, 0, None, None, norm_mix_pre[l], norm_mix_post[l], w_in[l], b_forget[l],
            lam_q1[l], lam_k1[l], lam_q2[l], lam_k2[l], diff_subln[l], w_branch[l], w_out[l],
            norm_ffn_pre[l], norm_ffn_post[l], w_up[l], conv_ffn[l], w_down[l], lam0)
        past = (
            cache_fox_k[l, page_table].reshape((nb, n_past) + cache_fox_k.shape[3:]),
            cache_fox_v[l, page_table].reshape((nb, n_past) + cache_fox_v.shape[3:]),
            cache_fox_logf[l, page_table].reshape((nb, n_past) + cache_fox_logf.shape[3:]),
            cache_diff_k[l, page_table].reshape((nb, n_past) + cache_diff_k.shape[3:]),
            cache_diff_v[l, page_table].reshape((nb, n_past) + cache_diff_v.shape[3:]),
        )
        xs, rows_s, conv_s = _layer(
            xs, n_past, past, state_ffn_conv[l], norm_mix_pre[l], norm_mix_post[l], w_in[l],
            b_forget[l], lam_q1[l], lam_k1[l], lam_q2[l], lam_k2[l], diff_subln[l], w_branch[l],
            w_out[l], norm_ffn_pre[l], norm_ffn_post[l], w_up[l], conv_ffn[l], w_down[l], lam0)
        p_fk.append(rows_p[0]); p_fv.append(rows_p[1]); p_lf.append(rows_p[2])
        p_dk.append(rows_p[3]); p_dv.append(rows_p[4]); p_cv.append(conv_p)
        s_fk.append(rows_s[0]); s_fv.append(rows_s[1]); s_lf.append(rows_s[2])
        s_dk.append(rows_s[3]); s_dv.append(rows_s[4]); s_cv.append(conv_s)
    return (xp, xs,
            jnp.stack(p_fk), jnp.stack(p_fv), jnp.stack(p_lf),
            jnp.stack(p_dk), jnp.stack(p_dv), jnp.stack(p_cv),
            jnp.stack(s_fk), jnp.stack(s_fv), jnp.stack(s_lf),
            jnp.stack(s_dk), jnp.stack(s_dv), jnp.stack(s_cv))
```

```python
import functools
import math

import jax
import jax.numpy as jnp
from jax import lax
from jax.experimental import pallas as pl
from jax.experimental.pallas import tpu as pltpu

HEAD_DIM = 64
CONV_WIDTH = 3
ROPE_THETA = 10000.0
RMS_EPS = 1e-6
NEG_INF = -1e30
Q_SCALE = HEAD_DIM ** -0.5

LANES = 128
SUBLANES = 8
V7X_VMEM_BYTES = 64 * 1024 * 1024
VMEM_CAP_BYTES = V7X_VMEM_BYTES - 8 * 1024 * 1024

F32 = jnp.float32
BF16 = jnp.bfloat16

ROW_TILE = 1024
COL_TILE = 1024
ATTN_TILE = 512
FF_TILE = 256
PAGES_PER_STEP = 8


def _vmem(nbytes):
    return int(min(VMEM_CAP_BYTES, max(32 * 1024 * 1024, 2 * nbytes)))


def _params(n_axes, vmem_bytes):
    return pltpu.CompilerParams(dimension_semantics=("arbitrary",) * n_axes,
                                vmem_limit_bytes=_vmem(vmem_bytes))


def _nt(a, b):
    return lax.dot_general(a, b, (((1,), (1,)), ((), ())), preferred_element_type=F32)


def _nn(a, b):
    return jnp.dot(a, b, preferred_element_type=F32)


def _rms(x, g):
    return x * lax.rsqrt(jnp.mean(x * x, axis=-1, keepdims=True) + RMS_EPS) * g


def _tile(n, pref):
    t = min(n, pref)
    while n % t:
        t //= 2
    return t


def _rmsnorm_kernel(x_ref, g_ref, o_ref):
    o_ref[...] = _rms(x_ref[...], g_ref[...]).astype(o_ref.dtype)


def rmsnorm_bf16(x, g, tm):
    t, d = x.shape
    return pl.pallas_call(
        _rmsnorm_kernel,
        grid=(t // tm,),
        in_specs=[pl.BlockSpec((tm, d), lambda i: (i, 0)),
                  pl.BlockSpec((1, d), lambda i: (0, 0))],
        out_specs=pl.BlockSpec((tm, d), lambda i: (i, 0)),
        out_shape=jax.ShapeDtypeStruct((t, d), BF16),
        compiler_params=_params(1, 2 * tm * d * 6),
        name="rmsnorm_bf16",
    )(x, g.reshape(1, d))


def _resnorm_kernel(x_ref, z_ref, g1_ref, o_ref):
    o_ref[...] = x_ref[...] + _rms(z_ref[...], g1_ref[...])


def _resnorm2_kernel(x_ref, z_ref, g1_ref, g2_ref, o_ref, h_ref):
    y = x_ref[...] + _rms(z_ref[...], g1_ref[...])
    o_ref[...] = y
    h_ref[...] = _rms(y, g2_ref[...]).astype(h_ref.dtype)


def residual_norm(x, z, g1, g2, tm):
    t, d = x.shape
    row = pl.BlockSpec((tm, d), lambda i: (i, 0))
    vec = pl.BlockSpec((1, d), lambda i: (0, 0))
    if g2 is None:
        return pl.pallas_call(
            _resnorm_kernel, grid=(t // tm,), in_specs=[row, row, vec], out_specs=row,
            out_shape=jax.ShapeDtypeStruct((t, d), F32),
            compiler_params=_params(1, 2 * tm * d * 12), name="residual_norm",
        )(x, z, g1.reshape(1, d))
    return pl.pallas_call(
        _resnorm2_kernel, grid=(t // tm,), in_specs=[row, row, vec, vec], out_specs=[row, row],
        out_shape=[jax.ShapeDtypeStruct((t, d), F32), jax.ShapeDtypeStruct((t, d), BF16)],
        compiler_params=_params(1, 2 * tm * d * 14), name="residual_norm2",
    )(x, z, g1.reshape(1, d), g2.reshape(1, d))


def _mm_kernel(*refs, n_extra, n_out, epilogue, w_transposed, feature_major):
    a_ref, w_ref = refs[0], refs[1]
    extra = refs[2:2 + n_extra]
    outs = refs[2 + n_extra:2 + n_extra + n_out]
    wb_ref = refs[-1]

    @pl.when(pl.program_id(1) == 0)
    def _():
        wb_ref[...] = w_ref[...].astype(BF16)

    a = a_ref[...].astype(BF16)
    if feature_major:
        acc = _nt(wb_ref[...], a)
    elif w_transposed:
        acc = _nt(a, wb_ref[...])
    else:
        acc = _nn(a, wb_ref[...])
    epilogue(acc, extra, outs)


def _epi_plain(scale):
    def epi(acc, extra, outs):
        val = acc if scale == 1.0 else acc * scale
        for o in outs:
            o[...] = val.astype(o.dtype).reshape(o.shape)
    return epi


def _epi_heads(acc, extra, outs):
    native, dense = outs
    for h in range(native.shape[1]):
        native[:, h, :] = acc[:, h * LANES:(h + 1) * LANES]
    dense[...] = acc.astype(dense.dtype)


def _epi_rope(scale):
    def epi(acc, extra, outs):
        cos = extra[0][...]
        sin = extra[1][...]
        first_half = (lax.broadcasted_iota(jnp.int32, (1, LANES), 1) % HEAD_DIM) < HEAD_DIM // 2
        for c in range(acc.shape[1] // LANES):
            x = acc[:, c * LANES:(c + 1) * LANES]
            swapped = jnp.where(first_half, pltpu.roll(x, LANES - HEAD_DIM // 2, 1),
                                pltpu.roll(x, HEAD_DIM // 2, 1))
            val = x * cos + swapped * sin
            if scale != 1.0:
                val = val * scale
            for o in outs:
                o[:, c * LANES:(c + 1) * LANES] = val.astype(o.dtype)
    return epi


def _epi_rope_fm(acc, extra, outs):
    cos = extra[0][...]
    sin = extra[1][...]
    half = HEAD_DIM // 2
    for m in range(acc.shape[0] // HEAD_DIM):
        x1 = acc[m * HEAD_DIM:m * HEAD_DIM + half]
        x2 = acc[m * HEAD_DIM + half:(m + 1) * HEAD_DIM]
        y1 = x1 * cos - x2 * sin
        y2 = x2 * cos + x1 * sin
        for o in outs:
            o[0, m * HEAD_DIM:m * HEAD_DIM + half, :] = y1.astype(o.dtype)
            o[0, m * HEAD_DIM + half:(m + 1) * HEAD_DIM, :] = y2.astype(o.dtype)


def matmul(a, w, *, w_transposed, col_block, n_blocks, tn, tm, outs, epilogue, extras=(),
           feature_major=False):
    t, k = a.shape
    in_specs = [pl.BlockSpec((tm, k), lambda j, i: (i, 0))]
    if w_transposed:
        in_specs.append(pl.BlockSpec((tn, k), lambda j, i: (col_block + j, 0)))
        w_scratch = pltpu.VMEM((tn, k), BF16)
    else:
        in_specs.append(pl.BlockSpec((k, tn), lambda j, i: (0, col_block + j)))
        w_scratch = pltpu.VMEM((k, tn), BF16)
    in_specs += [spec for _, spec in extras]
    nbytes = 2 * tm * k * a.dtype.itemsize + 2 * k * tn * 4 + k * tn * 2 + 2 * tm * tn * 4
    nbytes += sum(2 * tm * tn * s.dtype.itemsize for s, _ in outs)
    return pl.pallas_call(
        functools.partial(_mm_kernel, n_extra=len(extras), n_out=len(outs), epilogue=epilogue,
                          w_transposed=w_transposed, feature_major=feature_major),
        grid=(n_blocks, t // tm),
        in_specs=in_specs,
        out_specs=[spec for _, spec in outs],
        out_shape=[s for s, _ in outs],
        scratch_shapes=[w_scratch],
        compiler_params=_params(2, nbytes),
        name="proj_matmul",
    )(a, w, *[x for x, _ in extras])


def _token_out(t, n, tm, tn, dtype):
    return jax.ShapeDtypeStruct((t, n), dtype), pl.BlockSpec((tm, tn), lambda j, i: (i, j))


def _feature_out(nb, n, seq, tm, tn, dtype):
    tps = seq // tm
    return (jax.ShapeDtypeStruct((nb, n, seq), dtype),
            pl.BlockSpec((1, tn, tm), lambda j, i: (i // tps, j, i % tps)))


def _log_sigmoid(x):
    return jnp.minimum(x, 0.0) - jnp.log1p(jnp.exp(-jnp.abs(x)))


def _logf_kernel(a_ref, w_ref, b_ref, o_ref):
    acc = _nt(w_ref[...].astype(BF16), a_ref[...])
    o_ref[0] = _log_sigmoid(acc + b_ref[...])


def forget_gate(a, w_ff_t, b_forget, seq, tm):
    t, k = a.shape
    nh = w_ff_t.shape[0]
    tps = seq // tm
    return pl.pallas_call(
        _logf_kernel, grid=(t // tm,),
        in_specs=[pl.BlockSpec((tm, k), lambda i: (i, 0)),
                  pl.BlockSpec((nh, k), lambda i: (0, 0)),
                  pl.BlockSpec((nh, 1), lambda i: (0, 0))],
        out_specs=pl.BlockSpec((1, nh, tm), lambda i: (i // tps, 0, i % tps)),
        out_shape=jax.ShapeDtypeStruct((t // seq, nh, seq), F32),
        compiler_params=_params(1, 2 * tm * k * 2 + k * LANES * 8),
        name="forget_gate",
    )(a, w_ff_t, b_forget.reshape(nh, 1))


def _scan_lanes(x):
    n = x.shape[1]
    lane = lax.broadcasted_iota(jnp.int32, (1, n), 1)
    shift = 1
    while shift < n:
        x = x + jnp.where(lane >= shift, pltpu.roll(x, shift, 1), 0.0)
        shift *= 2
    return x


def _split3(c):
    hi = c.astype(BF16)
    r1 = c - hi.astype(F32)
    mid = r1.astype(BF16)
    lo = (r1 - mid.astype(F32)).astype(BF16)
    return hi, mid, lo


def _cumsum_prompt_kernel(lf_ref, ct_ref, c_ref, *, blk):
    c = _scan_lanes(lf_ref[0])
    ct_ref[0] = c
    eye = (lax.broadcasted_iota(jnp.int32, (blk, blk), 0)
           == lax.broadcasted_iota(jnp.int32, (blk, blk), 1)).astype(BF16)
    for j in range(c.shape[1] // blk):
        hi, mid, lo = _split3(c[:, j * blk:(j + 1) * blk])
        c_ref[j * blk:(j + 1) * blk, :] = _nt(eye, hi) + _nt(eye, mid) + _nt(eye, lo)


def cumsum_prompt(logf_t, blk):
    b, nh, s = logf_t.shape
    return pl.pallas_call(
        functools.partial(_cumsum_prompt_kernel, blk=blk), grid=(b,),
        in_specs=[pl.BlockSpec((1, nh, s), lambda i: (i, 0, 0))],
        out_specs=[pl.BlockSpec((1, nh, s), lambda i: (i, 0, 0)),
                   pl.BlockSpec((s, nh), lambda i: (i, 0))],
        out_shape=[jax.ShapeDtypeStruct((b, nh, s), F32), jax.ShapeDtypeStruct((b * s, nh), F32)],
        compiler_params=_params(1, 8 * s * LANES * 4),
        name="cumsum_prompt",
    )(logf_t)


def _cumsum_sample_kernel(pt_ref, lfn_ref, cache_ref, c_ref, buf, sem, *, n_pages, page, n_new):
    b = pl.program_id(0)

    def page_copy(p):
        return pltpu.make_async_copy(cache_ref.at[pt_ref[b, p]], buf.at[:, pl.ds(p * page, page)], sem.at[0])

    for p in range(n_pages):
        page_copy(p).start()
    n_tok = lfn_ref.shape[1]
    tok = lax.broadcasted_iota(jnp.int32, (n_tok, 1), 0)
    j = lax.broadcasted_iota(jnp.int32, (1, page), 1)
    sel = ((tok == b * n_new + j) & (j < n_new)).astype(BF16)
    hi, mid, lo = _split3(lfn_ref[...])
    buf[:, n_pages * page:] = _nn(hi, sel) + _nn(mid, sel) + _nn(lo, sel)
    for p in range(n_pages):
        page_copy(p).wait()
    c_ref[0] = _scan_lanes(buf[...])


def cumsum_sample(page_table, logf_new_t, cache_logf_t, n_new):
    nb, n_pages = page_table.shape
    _, nh, page = cache_logf_t.shape
    n_cols = n_pages * page + page
    grid_spec = pltpu.PrefetchScalarGridSpec(
        num_scalar_prefetch=1, grid=(nb,),
        in_specs=[pl.BlockSpec(logf_new_t.shape, lambda i, pt: (0, 0)),
                  pl.BlockSpec(memory_space=pl.ANY)],
        out_specs=pl.BlockSpec((1, nh, n_cols), lambda i, pt: (i, 0, 0)),
        scratch_shapes=[pltpu.VMEM((nh, n_cols), F32), pltpu.SemaphoreType.DMA((1,))])
    return pl.pallas_call(
        functools.partial(_cumsum_sample_kernel, n_pages=n_pages, page=page, n_new=n_new),
        grid_spec=grid_spec,
        out_shape=jax.ShapeDtypeStruct((nb, nh, n_cols), F32),
        compiler_params=_params(1, 24 * nh * n_cols * 4),
        name="cumsum_sample",
    )(page_table, logf_new_t, cache_logf_t)


def _flash_head(qm, kv_fn, i, tq, cq, ck_fn):
    rows = lax.broadcasted_iota(jnp.int32, (tq, tq), 0)
    cols = lax.broadcasted_iota(jnp.int32, (tq, tq), 1)

    def block(j, carry, diagonal):
        m, l, acc = carry
        r0 = pl.multiple_of(j * tq, tq)
        kt, pv = kv_fn(r0)
        s = _nn(qm, kt)
        if ck_fn is not None:
            s = s - ck_fn(r0)
        if diagonal:
            s = jnp.where(cols <= rows, s, NEG_INF)
        row_max = jnp.max(s, axis=1, keepdims=True)
        m_new = jnp.maximum(m, row_max if cq is None else cq + row_max)
        p = jnp.exp(s + ((-m_new) if cq is None else (cq - m_new)))
        alpha = jnp.exp(m - m_new)
        l = alpha * l + jnp.sum(p, axis=1, keepdims=True)
        acc = alpha * acc + pv(p.astype(BF16))
        return m_new, l, acc

    carry = (jnp.full((tq, 1), -jnp.inf, F32), jnp.zeros((tq, 1), F32), jnp.zeros((tq, LANES), F32))
    carry = lax.fori_loop(0, i, lambda j, c: block(j, c, False), carry)
    _, l, acc = block(i, carry, True)
    return acc, l


def _own_lanes(q2, a):
    lane = lax.broadcasted_iota(jnp.int32, (1, LANES), 1)
    own = (lane < HEAD_DIM) if a == 0 else (lane >= HEAD_DIM)
    return jnp.where(own, q2, jnp.zeros_like(q2))


def _fox_prompt_kernel(q_ref, k_ref, v_ref, c_ref, ct_ref, o_ref, *, tq, n_heads):
    i = pl.program_id(1)
    lane = lax.broadcasted_iota(jnp.int32, (1, LANES), 1)
    for hp in range(n_heads // 2):
        col = hp * LANES
        q2 = q_ref[:, col:col + LANES]

        def kv_fn(r0, col=col):
            vt = v_ref[0, col:col + LANES, pl.ds(r0, tq)]
            return k_ref[0, col:col + LANES, pl.ds(r0, tq)], lambda p: _nt(p, vt)

        res = []
        for a in range(2):
            h = 2 * hp + a
            cq = c_ref[:, h:h + 1]
            ck_fn = lambda r0, h=h: ct_ref[0, h:h + 1, pl.ds(r0, tq)]
            acc, l = _flash_head(_own_lanes(q2, a), kv_fn, i, tq, cq, ck_fn)
            res.append(acc / l)
        o_ref[:, col:col + LANES] = jnp.where(lane < HEAD_DIM, res[0], res[1]).astype(o_ref.dtype)


def _lambda(lq1, lk1, lq2, lk2, lam0):
    return (jnp.exp(jnp.sum(lq1 * lk1, axis=1, keepdims=True))
            - jnp.exp(jnp.sum(lq2 * lk2, axis=1, keepdims=True)) + lam0)


def _diff_prompt_kernel(q_ref, k_ref, v_ref, lq1, lk1, lq2, lk2, g_ref, o_ref, *, tq, n_heads, lam0):
    i = pl.program_id(1)
    lam = _lambda(lq1[...], lk1[...], lq2[...], lk2[...], lam0)
    for h in range(n_heads):
        col = h * LANES
        q2 = q_ref[:, col:col + LANES]

        def kv_fn(r0, col=col):
            vb = v_ref[pl.ds(r0, tq), col:col + LANES]
            return k_ref[0, col:col + LANES, pl.ds(r0, tq)], lambda p: _nn(p, vb)

        res = []
        for a in range(2):
            acc, l = _flash_head(_own_lanes(q2, a), kv_fn, i, tq, None, None)
            res.append(acc / l)
        o = res[0] - lam * res[1]
        o_ref[:, col:col + LANES] = (_rms(o, g_ref[...]) * (1.0 - lam0)).astype(o_ref.dtype)


def fox_prompt(q, k_t, v_t, c, c_t, tq):
    t, w = q.shape
    nb, _, s = k_t.shape
    nq = s // tq
    nh = c.shape[1]
    return pl.pallas_call(
        functools.partial(_fox_prompt_kernel, tq=tq, n_heads=nh),
        grid=(nb, nq),
        in_specs=[pl.BlockSpec((tq, w), lambda b, i: (b * nq + i, 0)),
                  pl.BlockSpec((1, w, s), lambda b, i: (b, 0, 0)),
                  pl.BlockSpec((1, w, s), lambda b, i: (b, 0, 0)),
                  pl.BlockSpec((tq, nh), lambda b, i: (b * nq + i, 0)),
                  pl.BlockSpec((1, nh, s), lambda b, i: (b, 0, 0))],
        out_specs=pl.BlockSpec((tq, w), lambda b, i: (b * nq + i, 0)),
        out_shape=jax.ShapeDtypeStruct((t, w), BF16),
        compiler_params=_params(2, 4 * s * w * 2 + 4 * tq * w * 2 + 16 * tq * tq * 4),
        name="fox_prompt",
    )(q, k_t, v_t, c, c_t)


def diff_prompt(q, k_t, v, lams, gain, lam0, tq):
    t, w = q.shape
    nb, _, s = k_t.shape
    nq = s // tq
    vw = v.shape[1]
    vec = pl.BlockSpec((1, HEAD_DIM), lambda b, i: (0, 0))
    return pl.pallas_call(
        functools.partial(_diff_prompt_kernel, tq=tq, n_heads=vw // LANES, lam0=lam0),
        grid=(nb, nq),
        in_specs=[pl.BlockSpec((tq, w), lambda b, i: (b * nq + i, 0)),
                  pl.BlockSpec((1, w, s), lambda b, i: (b, 0, 0)),
                  pl.BlockSpec((s, vw), lambda b, i: (b, 0)),
                  vec, vec, vec, vec,
                  pl.BlockSpec((1, LANES), lambda b, i: (0, 0))],
        out_specs=pl.BlockSpec((tq, vw), lambda b, i: (b * nq + i, 0)),
        out_shape=jax.ShapeDtypeStruct((t, vw), BF16),
        compiler_params=_params(2, 4 * s * w * 2 + 4 * tq * w * 2 + 16 * tq * tq * 4),
        name="diff_prompt",
    )(q, k_t, v, *[x.reshape(1, HEAD_DIM) for x in lams], gain.reshape(1, LANES))


def _sample_attn_kernel(pt_ref, *refs, fox, pages_per_step, n_steps, page, n_new, lam0):
    if fox:
        q_ref, kn_ref, vn_ref, c_ref, kc_ref, vc_ref, o_ref = refs[:7]
        scratch = refs[7:]
    else:
        q_ref, kn_ref, vn_ref, lq1, lk1, lq2, lk2, g_ref, kc_ref, vc_ref, o_ref = refs[:11]
        scratch = refs[11:]
    kbuf, vbuf, sem, qbd_ref, m_ref, l_ref, acc_ref, cq_ref = scratch

    b = pl.program_id(0)
    s = pl.program_id(1)
    nb = pl.num_programs(0)
    t = b * n_steps + s
    slot = t % 2
    w = q_ref.shape[1]
    n_rows = qbd_ref.shape[0]
    n_groups = n_rows // n_new
    n_heads_v = w // LANES
    n_past = n_steps * pages_per_step * page
    chunk = pages_per_step * page

    def copies(bb, ss, sl):
        out = []
        for p in range(pages_per_step):
            pg = pt_ref[bb, ss * pages_per_step + p]
            out.append(pltpu.make_async_copy(kc_ref.at[pg], kbuf.at[sl, :, pl.ds(p * page, page)],
                                             sem.at[sl, 0]))
            if fox:
                vdst = vbuf.at[sl, :, pl.ds(p * page, page)]
            else:
                vdst = vbuf.at[sl, pl.ds(p * page * n_heads_v, page * n_heads_v), :]
            out.append(pltpu.make_async_copy(vc_ref.at[pg], vdst, sem.at[sl, 1]))
        return out

    @pl.when(t == 0)
    def _():
        for cp in copies(b, s, slot):
            cp.start()

    @pl.when(t + 1 < nb * n_steps)
    def _():
        t1 = t + 1
        for cp in copies(t1 // n_steps, t1 % n_steps, t1 % 2):
            cp.start()

    row = lax.broadcasted_iota(jnp.int32, (n_rows, 1), 0)

    def bias(col0, width):
        cb = c_ref[0, :, pl.ds(col0, width)]
        return jnp.broadcast_to(cb[:, None, :], (n_groups, n_new, width)).reshape(n_rows, width)

    @pl.when(s == 0)
    def _():
        qt = jnp.concatenate([q_ref[...]] * n_groups, axis=0)
        colg = lax.broadcasted_iota(jnp.int32, (1, w), 1) // HEAD_DIM
        qbd_ref[...] = jnp.where(row // n_new == colg, qt, 0.0).astype(BF16)
        m_ref[...] = jnp.full(m_ref.shape, -jnp.inf, F32)
        l_ref[...] = jnp.zeros(l_ref.shape, F32)
        acc_ref[...] = jnp.zeros(acc_ref.shape, F32)
        if fox:
            lane = lax.broadcasted_iota(jnp.int32, (1, page), 1)
            cq_ref[...] = jnp.sum(jnp.where(lane == row % n_new, bias(n_past, page), 0.0),
                                  axis=1, keepdims=True)
        else:
            cq_ref[...] = jnp.zeros(cq_ref.shape, F32)

    def update(sc, pv):
        cq = cq_ref[...]
        m_prev = m_ref[...]
        m_new = jnp.maximum(m_prev, cq + jnp.max(sc, axis=1, keepdims=True))
        p = jnp.exp(sc + (cq - m_new))
        alpha = jnp.exp(m_prev - m_new)
        l_ref[...] = alpha * l_ref[...] + jnp.sum(p, axis=1, keepdims=True)
        acc_ref[...] = alpha * acc_ref[...] + pv(p.astype(BF16))
        m_ref[...] = m_new

    def pv_heads(p, v_of_head):
        rows_per_head = 2 * n_new
        return jnp.concatenate(
            [_nn(p[h * rows_per_head:(h + 1) * rows_per_head], v_of_head(h)) for h in range(n_heads_v)],
            axis=0)

    for cp in copies(b, s, slot):
        cp.wait()

    sc = _nn(qbd_ref[...], kbuf[slot].astype(BF16))
    if fox:
        sc = sc - bias(pl.multiple_of(s * chunk, chunk), chunk)
        vt = vbuf[slot].astype(BF16)
        update(sc, lambda p: _nt(p, vt))
    else:
        update(sc, lambda p: pv_heads(
            p, lambda h: vbuf[slot, pl.ds(h, chunk, stride=n_heads_v), :].astype(BF16)))

    @pl.when(s == n_steps - 1)
    def _():
        pad = jnp.zeros((page - n_new, w), F32)
        kn = jnp.concatenate([kn_ref[...], pad], axis=0).astype(BF16)
        vn = jnp.concatenate([vn_ref[...], pad], axis=0).astype(BF16)
        sn = _nt(qbd_ref[...], kn)
        if fox:
            sn = sn - bias(n_past, page)
        lane = lax.broadcasted_iota(jnp.int32, (1, page), 1)
        sn = jnp.where(lane <= row % n_new, sn, NEG_INF)
        if fox:
            update(sn, lambda p: _nn(p, vn))
        else:
            update(sn, lambda p: pv_heads(p, lambda h: vn[:, h * LANES:(h + 1) * LANES]))

        on = acc_ref[...] / l_ref[...]
        if fox:
            col = lax.broadcasted_iota(jnp.int32, (1, 1, w), 2)
            grp = lax.broadcasted_iota(jnp.int32, (n_groups, 1, 1), 0)
            o3 = on.reshape(n_groups, n_new, w)
            o_ref[...] = jnp.sum(jnp.where(grp == col // HEAD_DIM, o3, 0.0), axis=0)
        else:
            lam = _lambda(lq1[...], lk1[...], lq2[...], lk2[...], lam0)
            for h in range(n_heads_v):
                r0 = 2 * h * n_new
                oh = on[r0:r0 + n_new] - lam * on[r0 + n_new:r0 + 2 * n_new]
                o_ref[:, h * LANES:(h + 1) * LANES] = _rms(oh, g_ref[...]) * (1.0 - lam0)


def sample_attention(page_table, q, k_new, v_new, k_cache_t, v_cache, pages_per_step, *,
                     c_all=None, lams=None, gain=None, lam0=0.0):
    fox = c_all is not None
    nb, n_pages = page_table.shape
    _, w, page = k_cache_t.shape
    n_new = q.shape[0] // nb
    n_steps = n_pages // pages_per_step
    chunk = pages_per_step * page
    n_rows = (w // HEAD_DIM) * n_new
    new_spec = pl.BlockSpec((n_new, w), lambda b, s, pt: (b, 0))
    any_spec = pl.BlockSpec(memory_space=pl.ANY)
    if fox:
        ins = [q, k_new, v_new, c_all, k_cache_t, v_cache]
        in_specs = [new_spec, new_spec, new_spec,
                    pl.BlockSpec((1,) + c_all.shape[1:], lambda b, s, pt: (b, 0, 0)),
                    any_spec, any_spec]
        v_scratch = pltpu.VMEM((2, w, chunk), F32)
        acc_cols = w
    else:
        vec = pl.BlockSpec((1, HEAD_DIM), lambda b, s, pt: (0, 0))
        ins = ([q, k_new, v_new] + [x.reshape(1, HEAD_DIM) for x in lams]
               + [gain.reshape(1, LANES), k_cache_t, v_cache])
        in_specs = [new_spec, new_spec, new_spec, vec, vec, vec, vec,
                    pl.BlockSpec((1, LANES), lambda b, s, pt: (0, 0)), any_spec, any_spec]
        v_scratch = pltpu.VMEM((2, chunk * (w // LANES), LANES), F32)
        acc_cols = LANES
    grid_spec = pltpu.PrefetchScalarGridSpec(
        num_scalar_prefetch=1, grid=(nb, n_steps), in_specs=in_specs,
        out_specs=new_spec,
        scratch_shapes=[pltpu.VMEM((2, w, chunk), F32), v_scratch,
                        pltpu.SemaphoreType.DMA((2, 2)),
                        pltpu.VMEM((n_rows, w), BF16),
                        pltpu.VMEM((n_rows, 1), F32), pltpu.VMEM((n_rows, 1), F32),
                        pltpu.VMEM((n_rows, acc_cols), F32), pltpu.VMEM((n_rows, 1), F32)])
    nbytes = 4 * chunk * w * 4 + 2 * chunk * w * 2 + 4 * n_rows * chunk * 4
    if fox:
        nbytes += 2 * c_all.shape[1] * c_all.shape[2] * 4
    return pl.pallas_call(
        functools.partial(_sample_attn_kernel, fox=fox, pages_per_step=pages_per_step, n_steps=n_steps,
                          page=page, n_new=n_new, lam0=lam0),
        grid_spec=grid_spec,
        out_shape=jax.ShapeDtypeStruct((nb * n_new, w), F32),
        compiler_params=_params(2, nbytes),
        name="fox_sample" if fox else "diff_sample",
    )(page_table, *ins)


def _merge_kernel(fo_ref, do_ref, wf_ref, wd_ref, gf_ref, gd_ref, o_ref, wfb, wdb):
    @pl.when(pl.program_id(1) == 0)
    def _():
        wfb[...] = wf_ref[0].astype(BF16)
        wdb[...] = wd_ref[0].astype(BF16)

    pf = _nn(fo_ref[...].astype(BF16), wfb[...])
    pd = _nn(do_ref[...].astype(BF16), wdb[...])
    o_ref[...] = (jax.nn.sigmoid(gf_ref[...]) * pf + jax.nn.sigmoid(gd_ref[...]) * pd).astype(o_ref.dtype)


def branch_merge(fo, do, w_branch, gates, tm, tn):
    t, bw = fo.shape
    d = w_branch.shape[2]
    nn = d // tn
    nbytes = 4 * tm * bw * fo.dtype.itemsize + 4 * bw * tn * 4 + 2 * bw * tn * 2 + 6 * tm * tn * 4
    return pl.pallas_call(
        _merge_kernel, grid=(nn, t // tm),
        in_specs=[pl.BlockSpec((tm, bw), lambda j, i: (i, 0)),
                  pl.BlockSpec((tm, bw), lambda j, i: (i, 0)),
                  pl.BlockSpec((1, bw, tn), lambda j, i: (0, 0, j)),
                  pl.BlockSpec((1, bw, tn), lambda j, i: (1, 0, j)),
                  pl.BlockSpec((tm, tn), lambda j, i: (i, j)),
                  pl.BlockSpec((tm, tn), lambda j, i: (i, nn + j))],
        out_specs=pl.BlockSpec((tm, tn), lambda j, i: (i, j)),
        out_shape=jax.ShapeDtypeStruct((t, d), BF16),
        scratch_shapes=[pltpu.VMEM((bw, tn), BF16), pltpu.VMEM((bw, tn), BF16)],
        compiler_params=_params(2, nbytes),
        name="branch_merge",
    )(fo, do, w_branch, w_branch, gates, gates)


def _gelu_tanh(x):
    return 0.5 * x * (1.0 + jnp.tanh(math.sqrt(2.0 / math.pi) * (x + 0.044715 * (x * x * x))))


def _causal_conv(u, cw, prev1, prev2, period):
    n = u.shape[0]
    pos = lax.broadcasted_iota(jnp.int32, (n, 1), 0) % period
    u1 = jnp.where(pos == 0, prev1, pltpu.roll(u, 1, 0))
    u2 = jnp.where(pos == 0, prev2, jnp.where(pos == 1, prev1, pltpu.roll(u, 2, 0)))
    return cw[0:1] * u2 + cw[1:2] * u1 + cw[2:3] * u


def _ffn_prompt_kernel(h_ref, wg_ref, wv_ref, cwg_ref, cwv_ref, wd_ref, f_ref, cs_ref,
                       carry_ref, *, tiles_per_seq):
    i = pl.program_id(0)
    f = pl.program_id(1)
    tm = h_ref.shape[0]
    h = h_ref[...]
    ug = _nn(h, wg_ref[...].astype(BF16))
    uv = _nn(h, wv_ref[...].astype(BF16))

    @pl.when(i % tiles_per_seq == 0)
    def _():
        carry_ref[f] = jnp.zeros(carry_ref.shape[1:], F32)

    pg = carry_ref[f, 0]
    pv = carry_ref[f, 1]
    yg = _causal_conv(ug, cwg_ref[...], pg[SUBLANES - 1:SUBLANES], pg[SUBLANES - 2:SUBLANES - 1], tm)
    yv = _causal_conv(uv, cwv_ref[...], pv[SUBLANES - 1:SUBLANES], pv[SUBLANES - 2:SUBLANES - 1], tm)
    carry_ref[f, 0] = ug[tm - SUBLANES:tm]
    carry_ref[f, 1] = uv[tm - SUBLANES:tm]
    cs_ref[0, 0] = ug[tm - (CONV_WIDTH - 1):tm]
    cs_ref[0, 1] = uv[tm - (CONV_WIDTH - 1):tm]

    act = (_gelu_tanh(yg) * yv).astype(BF16)

    @pl.when(f == 0)
    def _():
        f_ref[...] = jnp.zeros(f_ref.shape, F32)

    f_ref[...] += _nn(act, wd_ref[...].astype(BF16))


def ffn_prompt(h, w_up, conv_w, w_down, seq, tm, tf):
    t, d = h.shape
    dff = w_down.shape[0]
    nf = dff // tf
    tiles_per_seq = seq // tm
    nbytes = (2 * tm * d * 2 + 4 * d * tf * 4 + 2 * tf * d * 4 + 2 * tm * d * 4
              + 2 * d * tf * 2 + tf * d * 2 + 10 * tm * tf * 4 + nf * 2 * SUBLANES * tf * 4)
    return pl.pallas_call(
        functools.partial(_ffn_prompt_kernel, tiles_per_seq=tiles_per_seq),
        grid=(t // tm, nf),
        in_specs=[pl.BlockSpec((tm, d), lambda i, f: (i, 0)),
                  pl.BlockSpec((d, tf), lambda i, f: (0, f)),
                  pl.BlockSpec((d, tf), lambda i, f: (0, nf + f)),
                  pl.BlockSpec((CONV_WIDTH, tf), lambda i, f: (0, f)),
                  pl.BlockSpec((CONV_WIDTH, tf), lambda i, f: (0, nf + f)),
                  pl.BlockSpec((tf, d), lambda i, f: (f, 0))],
        out_specs=[pl.BlockSpec((tm, d), lambda i, f: (i, 0)),
                   pl.BlockSpec((1, 2, CONV_WIDTH - 1, tf), lambda i, f: (i, 0, 0, f))],
        out_shape=[jax.ShapeDtypeStruct((t, d), F32),
                   jax.ShapeDtypeStruct((t // tm, 2, CONV_WIDTH - 1, dff), F32)],
        scratch_shapes=[pltpu.VMEM((nf, 2, SUBLANES, tf), F32)],
        compiler_params=_params(2, nbytes),
        name="ffn_prompt",
    )(h, w_up, w_up, conv_w, conv_w, w_down)


def _ffn_sample_kernel(h_ref, wg_ref, wv_ref, cwg_ref, cwv_ref, wd_ref, sg_ref, sv_ref, f_ref, cs_ref,
                       *, n_new):
    f = pl.program_id(0)
    t = h_ref.shape[0]
    nb = t // n_new
    tf = wg_ref.shape[1]
    h = h_ref[...]
    ug = _nn(h, wg_ref[...].astype(BF16))
    uv = _nn(h, wv_ref[...].astype(BF16))

    def rows_of(state, r):
        return jnp.broadcast_to(state[:, r:r + 1, :], (nb, n_new, tf)).reshape(t, tf)

    sg = sg_ref[...]
    sv = sv_ref[...]
    yg = _causal_conv(ug, cwg_ref[...], rows_of(sg, 1), rows_of(sg, 0), n_new)
    yv = _causal_conv(uv, cwv_ref[...], rows_of(sv, 1), rows_of(sv, 0), n_new)
    keep = CONV_WIDTH - 1
    cs_ref[0] = ug.reshape(nb, n_new, tf)[:, n_new - keep:, :]
    cs_ref[1] = uv.reshape(nb, n_new, tf)[:, n_new - keep:, :]

    act = (_gelu_tanh(yg) * yv).astype(BF16)

    @pl.when(f == 0)
    def _():
        f_ref[...] = jnp.zeros(f_ref.shape, F32)

    f_ref[...] += _nn(act, wd_ref[...].astype(BF16))


def ffn_sample(h, w_up, conv_w, w_down, state, n_new, tf):
    t, d = h.shape
    dff = w_down.shape[0]
    nf = dff // tf
    nb = t // n_new
    keep = CONV_WIDTH - 1
    nbytes = (2 * t * d * 2 + 4 * d * tf * 4 + 2 * tf * d * 4 + 2 * t * d * 4
              + 2 * d * tf * 2 + tf * d * 2 + 12 * t * tf * 4 + 8 * nb * SUBLANES * tf * 4)
    return pl.pallas_call(
        functools.partial(_ffn_sample_kernel, n_new=n_new),
        grid=(nf,),
        in_specs=[pl.BlockSpec((t, d), lambda f: (0, 0)),
                  pl.BlockSpec((d, tf), lambda f: (0, f)),
                  pl.BlockSpec((d, tf), lambda f: (0, nf + f)),
                  pl.BlockSpec((CONV_WIDTH, tf), lambda f: (0, f)),
                  pl.BlockSpec((CONV_WIDTH, tf), lambda f: (0, nf + f)),
                  pl.BlockSpec((tf, d), lambda f: (f, 0)),
                  pl.BlockSpec((nb, keep, tf), lambda f: (0, 0, f)),
                  pl.BlockSpec((nb, keep, tf), lambda f: (0, 0, nf + f))],
        out_specs=[pl.BlockSpec((t, d), lambda f: (0, 0)),
                   pl.BlockSpec((2, nb, keep, tf), lambda f: (0, 0, 0, f))],
        out_shape=[jax.ShapeDtypeStruct((t, d), F32),
                   jax.ShapeDtypeStruct((2, nb, keep, dff), F32)],
        compiler_params=_params(1, nbytes),
        name="ffn_sample",
    )(h, w_up, w_up, conv_w, conv_w, w_down, state, state)


def _lambda_init(layer_idx):
    return 0.8 - 0.6 * math.exp(-0.3 * layer_idx)


def _rope_angles(pos):
    half = HEAD_DIM // 2
    inv_freq = ROPE_THETA ** (-jnp.arange(half, dtype=F32) / half)
    ang = pos.astype(F32)[:, None] * inv_freq[None, :]
    return jnp.cos(ang), jnp.sin(ang)


def _rope_tables(pos):
    cos, sin = _rope_angles(pos)
    return (jnp.concatenate([cos, cos, cos, cos], axis=1),
            jnp.concatenate([-sin, sin, -sin, sin], axis=1))


def _mixer_tail(x, fo, do, gates, w_branch_l, w_out_l, g_post, g_ffn_pre, tm, tn):
    t, d = x.shape
    merged = branch_merge(fo, do, w_branch_l, gates, tm, tn)
    mix = matmul(merged, w_out_l, w_transposed=False, col_block=0, n_blocks=d // tn, tn=tn, tm=tm,
                 outs=[_token_out(t, d, tm, tn, F32)], epilogue=_epi_plain(1.0))[0]
    return residual_norm(x, mix, g_post, g_ffn_pre, _tile(t, 512))


def kernel(x_prompt, x_sample, cache_fox_k, cache_fox_v, cache_fox_logf, cache_diff_k, cache_diff_v,
           state_ffn_conv, page_table, norm_mix_pre, norm_mix_post, w_in, b_forget, lam_q1, lam_k1,
           lam_q2, lam_k2, diff_subln, w_branch, w_out, norm_ffn_pre, norm_ffn_post, w_up, conv_ffn,
           w_down):
    nbp, seq, d = x_prompt.shape
    nbs, n_new, _ = x_sample.shape
    depth, n_pool, page, fox_heads, _ = cache_fox_k.shape
    diff_heads = cache_diff_v.shape[3]
    fw = fox_heads * HEAD_DIM
    dqk = 2 * diff_heads * HEAD_DIM
    dvw = diff_heads * 2 * HEAD_DIM
    dff = w_down.shape[1]
    n_pages = page_table.shape[1]
    n_past = n_pages * page
    tp, ts = nbp * seq, nbs * n_new

    tm = _tile(seq, ROW_TILE)
    tq = _tile(seq, ATTN_TILE)
    tf = _tile(dff, FF_TILE)
    tn = _tile(fw, COL_TILE)
    tn_d = _tile(d, COL_TILE)
    pages_per_step = _tile(n_pages, PAGES_PER_STEP)
    assert dqk == fw and dvw == fw and fw % tn == 0 and (2 * d) % tn == 0

    pos_p = jnp.tile(jnp.arange(seq), nbp)
    cos_p, sin_p = _rope_tables(pos_p)
    cos_pt, sin_pt = [x.T for x in _rope_angles(pos_p)]
    cos_s, sin_s = _rope_tables(n_past + jnp.tile(jnp.arange(n_new), nbs))

    def rope_extras(cos, sin, rows):
        spec = pl.BlockSpec((rows, LANES), lambda j, i: (i, 0))
        return [(cos, spec), (sin, spec)]

    xp = x_prompt.reshape(tp, d)
    xs = x_sample.reshape(ts, d)
    outs_p = [[] for _ in range(6)]
    outs_s = [[] for _ in range(6)]
    nblk = fw // tn
    for l in range(depth):
        lam0 = _lambda_init(l)
        lams = (lam_q1[l], lam_k1[l], lam_q2[l], lam_k2[l])
        w_t = w_in[l].T
        w_ff_t = w_t[3 * fw:3 * fw + fox_heads]
        w_rest_t = w_t[3 * fw + fox_heads:]
        proj = functools.partial(matmul, w_transposed=True, n_blocks=nblk, tn=tn)

        hp = rmsnorm_bf16(xp, norm_mix_pre[l], tm)
        pj = functools.partial(proj, hp, tm=tm)
        fm_pair = [_feature_out(nbp, fw, seq, tm, tn, F32), _feature_out(nbp, fw, seq, tm, tn, BF16)]
        fq = pj(w_t, col_block=0, outs=[_token_out(tp, fw, tm, tn, BF16)], epilogue=_epi_plain(Q_SCALE))[0]
        fk, fkb = pj(w_t, col_block=nblk, outs=fm_pair, epilogue=_epi_plain(1.0), feature_major=True)
        fv, fvb = pj(w_t, col_block=2 * nblk, outs=fm_pair, epilogue=_epi_plain(1.0), feature_major=True)
        logf = forget_gate(hp, w_ff_t, b_forget[l], seq, tm)
        dq = pj(w_rest_t, col_block=0, outs=[_token_out(tp, dqk, tm, tn, BF16)],
                epilogue=_epi_rope(Q_SCALE), extras=rope_extras(cos_p, sin_p, tm))[0]
        tps = seq // tm
        fm_spec = pl.BlockSpec((HEAD_DIM // 2, tm), lambda j, i: (0, i))
        dk, dkb = pj(w_rest_t, col_block=nblk, outs=fm_pair, epilogue=_epi_rope_fm,
                     extras=[(cos_pt, fm_spec), (sin_pt, fm_spec)], feature_major=True)
        dv, dvb = pj(w_rest_t, col_block=2 * nblk,
                     outs=[(jax.ShapeDtypeStruct((tp, dvw // LANES, LANES), F32),
                            pl.BlockSpec((tm, tn // LANES, LANES), lambda j, i: (i, j, 0))),
                           _token_out(tp, dvw, tm, tn, BF16)],
                     epilogue=_epi_heads)
        gates = matmul(hp, w_rest_t, w_transposed=True, col_block=3 * nblk, n_blocks=2 * d // tn, tn=tn,
                       tm=tm, outs=[_token_out(tp, 2 * d, tm, tn, F32)], epilogue=_epi_plain(1.0))[0]
        c_t, c = cumsum_prompt(logf, tq)
        fo = fox_prompt(fq, fkb, fvb, c, c_t, tq)
        do = diff_prompt(dq, dkb, dvb, lams, diff_subln[l], lam0, tq)
        x2, h2 = _mixer_tail(xp, fo, do, gates, w_branch[l], w_out[l], norm_mix_post[l], norm_ffn_pre[l],
                             tm, tn_d)
        f, conv_p = ffn_prompt(h2, w_up[l], conv_ffn[l], w_down[l], seq, tm, tf)
        xp = residual_norm(x2, f, norm_ffn_post[l], None, _tile(tp, 512))
        conv_p = conv_p[tps - 1::tps]
        conv_p = jnp.transpose(conv_p, (0, 2, 1, 3)).reshape(nbp, CONV_WIDTH - 1, 2 * dff)

        def heads_last(x_t, n_heads):
            return jnp.transpose(x_t.reshape(nbp, n_heads, HEAD_DIM, seq), (0, 3, 1, 2))

        for lst, val in zip(outs_p, (heads_last(fk, fox_heads), heads_last(fv, fox_heads),
                                     jnp.transpose(logf, (0, 2, 1)),
                                     heads_last(dk, 2 * diff_heads),
                                     dv.reshape(nbp, seq, diff_heads, 2 * HEAD_DIM), conv_p)):
            lst.append(val)

        hs = rmsnorm_bf16(xs, norm_mix_pre[l], ts)
        pj = functools.partial(proj, hs, tm=ts)
        tok = lambda n: [_token_out(ts, n, ts, tn, F32)]
        fq = pj(w_t, col_block=0, outs=tok(fw), epilogue=_epi_plain(Q_SCALE))[0]
        fk = pj(w_t, col_block=nblk, outs=tok(fw), epilogue=_epi_plain(1.0))[0]
        fv = pj(w_t, col_block=2 * nblk, outs=tok(fw), epilogue=_epi_plain(1.0))[0]
        logf = forget_gate(hs, w_ff_t, b_forget[l], ts, ts)[0]
        dq = pj(w_rest_t, col_block=0, outs=tok(dqk), epilogue=_epi_rope(Q_SCALE),
                extras=rope_extras(cos_s, sin_s, ts))[0]
        dk = pj(w_rest_t, col_block=nblk, outs=tok(dqk), epilogue=_epi_rope(1.0),
                extras=rope_extras(cos_s, sin_s, ts))[0]
        dv = pj(w_rest_t, col_block=2 * nblk, outs=tok(dvw), epilogue=_epi_plain(1.0))[0]
        gates = matmul(hs, w_rest_t, w_transposed=True, col_block=3 * nblk, n_blocks=2 * d // tn, tn=tn,
                       tm=ts, outs=[_token_out(ts, 2 * d, ts, tn, F32)], epilogue=_epi_plain(1.0))[0]
        feature_major = lambda cache: jnp.transpose(cache, (0, 2, 3, 1)).reshape(n_pool, -1, page)
        c_all = cumsum_sample(page_table, logf, jnp.transpose(cache_fox_logf[l], (0, 2, 1)), n_new)
        fo = sample_attention(page_table, fq, fk, fv, feature_major(cache_fox_k[l]),
                              feature_major(cache_fox_v[l]), pages_per_step, c_all=c_all)
        do = sample_attention(page_table, dq, dk, dv, feature_major(cache_diff_k[l]),
                              cache_diff_v[l].reshape(n_pool, page * diff_heads, 2 * HEAD_DIM),
                              pages_per_step, lams=lams, gain=diff_subln[l], lam0=lam0)
        x2, h2 = _mixer_tail(xs, fo, do, gates, w_branch[l], w_out[l], norm_mix_post[l], norm_ffn_pre[l],
                             ts, tn_d)
        f, conv_s = ffn_sample(h2, w_up[l], conv_ffn[l], w_down[l], state_ffn_conv[l], n_new, tf)
        xs = residual_norm(x2, f, norm_ffn_post[l], None, ts)
        conv_s = jnp.transpose(conv_s, (1, 2, 0, 3)).reshape(nbs, CONV_WIDTH - 1, 2 * dff)
        for lst, val in zip(outs_s, (fk.reshape(nbs, n_new, fox_heads, HEAD_DIM),
                                     fv.reshape(nbs, n_new, fox_heads, HEAD_DIM),
                                     logf.T.reshape(nbs, n_new, fox_heads),
                                     dk.reshape(nbs, n_new, 2 * diff_heads, HEAD_DIM),
                                     dv.reshape(nbs, n_new, diff_heads, 2 * HEAD_DIM), conv_s)):
            lst.append(val)

    return (xp.reshape(nbp, seq, d), xs.reshape(nbs, n_new, d),
            *[jnp.stack(v) for v in outs_p], *[jnp.stack(v) for v in outs_s])
```

```python
import functools
import math

import jax
import jax.numpy as jnp
from jax import lax
from jax.experimental import pallas as pl
from jax.experimental.pallas import tpu as pltpu

HEAD_DIM = 64
CONV_WIDTH = 3
ROPE_THETA = 10000.0
RMS_EPS = 1e-6
NEG_INF = -1e30
Q_SCALE = HEAD_DIM ** -0.5

LANES = 128
SUBLANES = 8
V7X_VMEM_BYTES = 64 * 1024 * 1024
VMEM_CAP_BYTES = V7X_VMEM_BYTES - 8 * 1024 * 1024

F32 = jnp.float32
BF16 = jnp.bfloat16

ROW_TILE = 1024
COL_TILE = 1024
ATTN_TILE = 512
FF_TILE = 1024
FF_SUB = 256
PAGES_PER_STEP = 8


def _vmem(nbytes):
    return int(min(VMEM_CAP_BYTES, max(32 * 1024 * 1024, 2 * nbytes)))


def _params(n_axes, vmem_bytes):
    return pltpu.CompilerParams(dimension_semantics=("arbitrary",) * n_axes,
                                vmem_limit_bytes=_vmem(vmem_bytes))


def _nt(a, b):
    return lax.dot_general(a, b, (((1,), (1,)), ((), ())), preferred_element_type=F32)


def _nn(a, b):
    return jnp.dot(a, b, preferred_element_type=F32)


def _rms(x, g):
    return x * lax.rsqrt(jnp.mean(x * x, axis=-1, keepdims=True) + RMS_EPS) * g


def _tile(n, pref):
    t = min(n, pref)
    while n % t:
        t //= 2
    return t


def _rmsnorm_kernel(x_ref, g_ref, o_ref):
    o_ref[...] = _rms(x_ref[...], g_ref[...]).astype(o_ref.dtype)


def rmsnorm_bf16(x, g, tm):
    t, d = x.shape
    return pl.pallas_call(
        _rmsnorm_kernel,
        grid=(t // tm,),
        in_specs=[pl.BlockSpec((tm, d), lambda i: (i, 0)),
                  pl.BlockSpec((1, d), lambda i: (0, 0))],
        out_specs=pl.BlockSpec((tm, d), lambda i: (i, 0)),
        out_shape=jax.ShapeDtypeStruct((t, d), BF16),
        compiler_params=_params(1, 2 * tm * d * 6),
        name="rmsnorm_bf16",
    )(x, g.reshape(1, d))


def _resnorm_kernel(x_ref, z_ref, g1_ref, o_ref):
    o_ref[...] = x_ref[...] + _rms(z_ref[...], g1_ref[...])


def _resnorm2_kernel(x_ref, z_ref, g1_ref, g2_ref, o_ref, h_ref):
    y = x_ref[...] + _rms(z_ref[...], g1_ref[...])
    o_ref[...] = y
    h_ref[...] = _rms(y, g2_ref[...]).astype(h_ref.dtype)


def residual_norm(x, z, g1, g2, tm):
    t, d = x.shape
    row = pl.BlockSpec((tm, d), lambda i: (i, 0))
    vec = pl.BlockSpec((1, d), lambda i: (0, 0))
    if g2 is None:
        return pl.pallas_call(
            _resnorm_kernel, grid=(t // tm,), in_specs=[row, row, vec], out_specs=row,
            out_shape=jax.ShapeDtypeStruct((t, d), F32),
            compiler_params=_params(1, 2 * tm * d * 12), name="residual_norm",
        )(x, z, g1.reshape(1, d))
    return pl.pallas_call(
        _resnorm2_kernel, grid=(t // tm,), in_specs=[row, row, vec, vec], out_specs=[row, row],
        out_shape=[jax.ShapeDtypeStruct((t, d), F32), jax.ShapeDtypeStruct((t, d), BF16)],
        compiler_params=_params(1, 2 * tm * d * 14), name="residual_norm2",
    )(x, z, g1.reshape(1, d), g2.reshape(1, d))


def _mm_kernel(*refs, n_extra, n_out, epilogue, w_transposed, feature_major):
    a_ref, w_ref = refs[0], refs[1]
    extra = refs[2:2 + n_extra]
    outs = refs[2 + n_extra:2 + n_extra + n_out]
    wb_ref = refs[-1]

    @pl.when(pl.program_id(1) == 0)
    def _():
        wb_ref[...] = w_ref[...].astype(BF16)

    a = a_ref[...].astype(BF16)
    if feature_major:
        acc = _nt(wb_ref[...], a)
    elif w_transposed:
        acc = _nt(a, wb_ref[...])
    else:
        acc = _nn(a, wb_ref[...])
    epilogue(acc, extra, outs)


def _epi_plain(scale):
    def epi(acc, extra, outs):
        val = acc if scale == 1.0 else acc * scale
        for o in outs:
            o[...] = val.astype(o.dtype).reshape(o.shape)
    return epi


def _epi_heads(acc, extra, outs):
    native, dense = outs
    for h in range(native.shape[1]):
        native[:, h, :] = acc[:, h * LANES:(h + 1) * LANES]
    dense[...] = acc.astype(dense.dtype)


def _epi_rope(scale):
    def epi(acc, extra, outs):
        cos = extra[0][...]
        sin = extra[1][...]
        first_half = (lax.broadcasted_iota(jnp.int32, (1, LANES), 1) % HEAD_DIM) < HEAD_DIM // 2
        for c in range(acc.shape[1] // LANES):
            x = acc[:, c * LANES:(c + 1) * LANES]
            swapped = jnp.where(first_half, pltpu.roll(x, LANES - HEAD_DIM // 2, 1),
                                pltpu.roll(x, HEAD_DIM // 2, 1))
            val = x * cos + swapped * sin
            if scale != 1.0:
                val = val * scale
            for o in outs:
                o[:, c * LANES:(c + 1) * LANES] = val.astype(o.dtype)
    return epi


def _epi_rope_fm(acc, extra, outs):
    cos = extra[0][...]
    sin = extra[1][...]
    half = HEAD_DIM // 2
    for m in range(acc.shape[0] // HEAD_DIM):
        x1 = acc[m * HEAD_DIM:m * HEAD_DIM + half]
        x2 = acc[m * HEAD_DIM + half:(m + 1) * HEAD_DIM]
        y1 = x1 * cos - x2 * sin
        y2 = x2 * cos + x1 * sin
        for o in outs:
            o[0, m * HEAD_DIM:m * HEAD_DIM + half, :] = y1.astype(o.dtype)
            o[0, m * HEAD_DIM + half:(m + 1) * HEAD_DIM, :] = y2.astype(o.dtype)


def matmul(a, w, *, w_transposed, col_block, n_blocks, tn, tm, outs, epilogue, extras=(),
           feature_major=False):
    t, k = a.shape
    in_specs = [pl.BlockSpec((tm, k), lambda j, i: (i, 0))]
    if w_transposed:
        in_specs.append(pl.BlockSpec((tn, k), lambda j, i: (col_block + j, 0)))
        w_scratch = pltpu.VMEM((tn, k), BF16)
    else:
        in_specs.append(pl.BlockSpec((k, tn), lambda j, i: (0, col_block + j)))
        w_scratch = pltpu.VMEM((k, tn), BF16)
    in_specs += [spec for _, spec in extras]
    nbytes = 2 * tm * k * a.dtype.itemsize + 2 * k * tn * 4 + k * tn * 2 + 2 * tm * tn * 4
    nbytes += sum(2 * tm * tn * s.dtype.itemsize for s, _ in outs)
    return pl.pallas_call(
        functools.partial(_mm_kernel, n_extra=len(extras), n_out=len(outs), epilogue=epilogue,
                          w_transposed=w_transposed, feature_major=feature_major),
        grid=(n_blocks, t // tm),
        in_specs=in_specs,
        out_specs=[spec for _, spec in outs],
        out_shape=[s for s, _ in outs],
        scratch_shapes=[w_scratch],
        compiler_params=_params(2, nbytes),
        name="proj_matmul",
    )(a, w, *[x for x, _ in extras])


def _token_out(t, n, tm, tn, dtype):
    return jax.ShapeDtypeStruct((t, n), dtype), pl.BlockSpec((tm, tn), lambda j, i: (i, j))


def _feature_out(nb, n, seq, tm, tn, dtype):
    tps = seq // tm
    return (jax.ShapeDtypeStruct((nb, n, seq), dtype),
            pl.BlockSpec((1, tn, tm), lambda j, i: (i // tps, j, i % tps)))


def _log_sigmoid(x):
    return jnp.minimum(x, 0.0) - jnp.log1p(jnp.exp(-jnp.abs(x)))


def _logf_kernel(a_ref, w_ref, b_ref, o_ref):
    acc = _nt(w_ref[...].astype(BF16), a_ref[...])
    o_ref[0] = _log_sigmoid(acc + b_ref[...])


def forget_gate(a, w_ff_t, b_forget, seq, tm):
    t, k = a.shape
    nh = w_ff_t.shape[0]
    tps = seq // tm
    return pl.pallas_call(
        _logf_kernel, grid=(t // tm,),
        in_specs=[pl.BlockSpec((tm, k), lambda i: (i, 0)),
                  pl.BlockSpec((nh, k), lambda i: (0, 0)),
                  pl.BlockSpec((nh, 1), lambda i: (0, 0))],
        out_specs=pl.BlockSpec((1, nh, tm), lambda i: (i // tps, 0, i % tps)),
        out_shape=jax.ShapeDtypeStruct((t // seq, nh, seq), F32),
        compiler_params=_params(1, 2 * tm * k * 2 + k * LANES * 8),
        name="forget_gate",
    )(a, w_ff_t, b_forget.reshape(nh, 1))


def _scan_lanes(x):
    n = x.shape[1]
    lane = lax.broadcasted_iota(jnp.int32, (1, n), 1)
    shift = 1
    while shift < n:
        x = x + jnp.where(lane >= shift, pltpu.roll(x, shift, 1), 0.0)
        shift *= 2
    return x


def _split3(c):
    hi = c.astype(BF16)
    r1 = c - hi.astype(F32)
    mid = r1.astype(BF16)
    lo = (r1 - mid.astype(F32)).astype(BF16)
    return hi, mid, lo


def _cumsum_prompt_kernel(lf_ref, ct_ref, c_ref, *, blk):
    c = _scan_lanes(lf_ref[0])
    ct_ref[0] = c
    eye = (lax.broadcasted_iota(jnp.int32, (blk, blk), 0)
           == lax.broadcasted_iota(jnp.int32, (blk, blk), 1)).astype(BF16)
    for j in range(c.shape[1] // blk):
        hi, mid, lo = _split3(c[:, j * blk:(j + 1) * blk])
        c_ref[j * blk:(j + 1) * blk, :] = _nt(eye, hi) + _nt(eye, mid) + _nt(eye, lo)


def cumsum_prompt(logf_t, blk):
    b, nh, s = logf_t.shape
    return pl.pallas_call(
        functools.partial(_cumsum_prompt_kernel, blk=blk), grid=(b,),
        in_specs=[pl.BlockSpec((1, nh, s), lambda i: (i, 0, 0))],
        out_specs=[pl.BlockSpec((1, nh, s), lambda i: (i, 0, 0)),
                   pl.BlockSpec((s, nh), lambda i: (i, 0))],
        out_shape=[jax.ShapeDtypeStruct((b, nh, s), F32), jax.ShapeDtypeStruct((b * s, nh), F32)],
        compiler_params=_params(1, 8 * s * LANES * 4),
        name="cumsum_prompt",
    )(logf_t)


def _cumsum_sample_kernel(pt_ref, lfn_ref, cache_ref, c_ref, buf, sem, *, n_pages, page, n_new):
    b = pl.program_id(0)

    def page_copy(p):
        return pltpu.make_async_copy(cache_ref.at[pt_ref[b, p]], buf.at[:, pl.ds(p * page, page)], sem.at[0])

    for p in range(n_pages):
        page_copy(p).start()
    n_tok = lfn_ref.shape[1]
    tok = lax.broadcasted_iota(jnp.int32, (n_tok, 1), 0)
    j = lax.broadcasted_iota(jnp.int32, (1, page), 1)
    sel = ((tok == b * n_new + j) & (j < n_new)).astype(BF16)
    hi, mid, lo = _split3(lfn_ref[...])
    buf[:, n_pages * page:] = _nn(hi, sel) + _nn(mid, sel) + _nn(lo, sel)
    for p in range(n_pages):
        page_copy(p).wait()
    c_ref[0] = _scan_lanes(buf[...])


def cumsum_sample(page_table, logf_new_t, cache_logf_t, n_new):
    nb, n_pages = page_table.shape
    _, nh, page = cache_logf_t.shape
    n_cols = n_pages * page + page
    grid_spec = pltpu.PrefetchScalarGridSpec(
        num_scalar_prefetch=1, grid=(nb,),
        in_specs=[pl.BlockSpec(logf_new_t.shape, lambda i, pt: (0, 0)),
                  pl.BlockSpec(memory_space=pl.ANY)],
        out_specs=pl.BlockSpec((1, nh, n_cols), lambda i, pt: (i, 0, 0)),
        scratch_shapes=[pltpu.VMEM((nh, n_cols), F32), pltpu.SemaphoreType.DMA((1,))])
    return pl.pallas_call(
        functools.partial(_cumsum_sample_kernel, n_pages=n_pages, page=page, n_new=n_new),
        grid_spec=grid_spec,
        out_shape=jax.ShapeDtypeStruct((nb, nh, n_cols), F32),
        compiler_params=_params(1, 24 * nh * n_cols * 4),
        name="cumsum_sample",
    )(page_table, logf_new_t, cache_logf_t)


def _flash_pair(q2, kv_fn, i, tq, cqs, ck_fns):
    rows = lax.broadcasted_iota(jnp.int32, (tq, tq), 0)
    cols = lax.broadcasted_iota(jnp.int32, (tq, tq), 1)
    lane = lax.broadcasted_iota(jnp.int32, (1, LANES), 1)
    qms = [jnp.where(lane < HEAD_DIM, q2, jnp.zeros_like(q2)),
           jnp.where(lane >= HEAD_DIM, q2, jnp.zeros_like(q2))]

    def block(j, carry, diagonal):
        r0 = pl.multiple_of(j * tq, tq)
        kt, pv = kv_fn(r0)
        out = []
        for a in range(2):
            m, l, acc = carry[a]
            s = _nn(qms[a], kt)
            if ck_fns is not None:
                s = s - ck_fns[a](r0)
            if diagonal:
                s = jnp.where(cols <= rows, s, NEG_INF)
            row_max = jnp.max(s, axis=1, keepdims=True)
            m_new = jnp.maximum(m, row_max if cqs is None else cqs[a] + row_max)
            p = jnp.exp(s + ((-m_new) if cqs is None else (cqs[a] - m_new)))
            alpha = jnp.exp(m - m_new)
            l = alpha * l + jnp.sum(p, axis=1, keepdims=True)
            acc = alpha * acc + pv(p.astype(BF16))
            out.append((m_new, l, acc))
        return tuple(out)

    init = (jnp.full((tq, 1), -jnp.inf, F32), jnp.zeros((tq, 1), F32), jnp.zeros((tq, LANES), F32))
    carry = lax.fori_loop(0, i, lambda j, c: block(j, c, False), (init, init))
    carry = block(i, carry, True)
    return [(acc, l) for _, l, acc in carry]


def _fox_prompt_kernel(q_ref, k_ref, v_ref, c_ref, ct_ref, o_ref, *, tq, n_heads):
    i = pl.program_id(1)
    lane = lax.broadcasted_iota(jnp.int32, (1, LANES), 1)
    head = lax.broadcasted_iota(jnp.int32, (1, n_heads), 1)
    c_blk = c_ref[...]
    for hp in range(n_heads // 2):
        col = hp * LANES

        def kv_fn(r0, col=col):
            vt = v_ref[0, col:col + LANES, pl.ds(r0, tq)]
            return k_ref[0, col:col + LANES, pl.ds(r0, tq)], lambda p: _nt(p, vt)

        hs = (2 * hp, 2 * hp + 1)
        cqs = [jnp.sum(jnp.where(head == h, c_blk, 0.0), axis=1, keepdims=True) for h in hs]
        ck_fns = [lambda r0, h=h: ct_ref[0, h:h + 1, pl.ds(r0, tq)] for h in hs]
        (acc0, l0), (acc1, l1) = _flash_pair(q_ref[:, col:col + LANES], kv_fn, i, tq, cqs, ck_fns)
        o_ref[:, col:col + LANES] = jnp.where(lane < HEAD_DIM, acc0 / l0, acc1 / l1).astype(o_ref.dtype)


def _lambda(lq1, lk1, lq2, lk2, lam0):
    return (jnp.exp(jnp.sum(lq1 * lk1, axis=1, keepdims=True))
            - jnp.exp(jnp.sum(lq2 * lk2, axis=1, keepdims=True)) + lam0)


def _diff_prompt_kernel(q_ref, k_ref, v_ref, lq1, lk1, lq2, lk2, g_ref, o_ref, *, tq, n_heads, lam0):
    i = pl.program_id(1)
    lam = _lambda(lq1[...], lk1[...], lq2[...], lk2[...], lam0)
    for h in range(n_heads):
        col = h * LANES

        def kv_fn(r0, col=col):
            vb = v_ref[pl.ds(r0, tq), col:col + LANES]
            return k_ref[0, col:col + LANES, pl.ds(r0, tq)], lambda p: _nn(p, vb)

        (acc0, l0), (acc1, l1) = _flash_pair(q_ref[:, col:col + LANES], kv_fn, i, tq, None, None)
        o = acc0 / l0 - lam * (acc1 / l1)
        o_ref[:, col:col + LANES] = (_rms(o, g_ref[...]) * (1.0 - lam0)).astype(o_ref.dtype)


def fox_prompt(q, k_t, v_t, c, c_t, tq):
    t, w = q.shape
    nb, _, s = k_t.shape
    nq = s // tq
    nh = c.shape[1]
    return pl.pallas_call(
        functools.partial(_fox_prompt_kernel, tq=tq, n_heads=nh),
        grid=(nb, nq),
        in_specs=[pl.BlockSpec((tq, w), lambda b, i: (b * nq + i, 0)),
                  pl.BlockSpec((1, w, s), lambda b, i: (b, 0, 0)),
                  pl.BlockSpec((1, w, s), lambda b, i: (b, 0, 0)),
                  pl.BlockSpec((tq, nh), lambda b, i: (b * nq + i, 0)),
                  pl.BlockSpec((1, nh, s), lambda b, i: (b, 0, 0))],
        out_specs=pl.BlockSpec((tq, w), lambda b, i: (b * nq + i, 0)),
        out_shape=jax.ShapeDtypeStruct((t, w), BF16),
        compiler_params=_params(2, 4 * s * w * 2 + 4 * tq * w * 2 + 16 * tq * tq * 4),
        name="fox_prompt",
    )(q, k_t, v_t, c, c_t)


def diff_prompt(q, k_t, v, lams, gain, lam0, tq):
    t, w = q.shape
    nb, _, s = k_t.shape
    nq = s // tq
    vw = v.shape[1]
    vec = pl.BlockSpec((1, HEAD_DIM), lambda b, i: (0, 0))
    return pl.pallas_call(
        functools.partial(_diff_prompt_kernel, tq=tq, n_heads=vw // LANES, lam0=lam0),
        grid=(nb, nq),
        in_specs=[pl.BlockSpec((tq, w), lambda b, i: (b * nq + i, 0)),
                  pl.BlockSpec((1, w, s), lambda b, i: (b, 0, 0)),
                  pl.BlockSpec((s, vw), lambda b, i: (b, 0)),
                  vec, vec, vec, vec,
                  pl.BlockSpec((1, LANES), lambda b, i: (0, 0))],
        out_specs=pl.BlockSpec((tq, vw), lambda b, i: (b * nq + i, 0)),
        out_shape=jax.ShapeDtypeStruct((t, vw), BF16),
        compiler_params=_params(2, 4 * s * w * 2 + 4 * tq * w * 2 + 16 * tq * tq * 4),
        name="diff_prompt",
    )(q, k_t, v, *[x.reshape(1, HEAD_DIM) for x in lams], gain.reshape(1, LANES))


def _sample_attn_kernel(pt_ref, *refs, fox, pages_per_step, n_steps, page, n_new, lam0):
    if fox:
        q_ref, kn_ref, vn_ref, c_ref, kc_ref, vc_ref, o_ref = refs[:7]
        scratch = refs[7:]
    else:
        q_ref, kn_ref, vn_ref, lq1, lk1, lq2, lk2, g_ref, kc_ref, vc_ref, o_ref = refs[:11]
        scratch = refs[11:]
    kbuf, vbuf, sem, qbd_ref, m_ref, l_ref, acc_ref, cq_ref = scratch

    b = pl.program_id(0)
    s = pl.program_id(1)
    nb = pl.num_programs(0)
    t = b * n_steps + s
    slot = t % 2
    w = q_ref.shape[1]
    n_rows = qbd_ref.shape[0]
    n_groups = n_rows // n_new
    n_heads_v = w // LANES
    n_past = n_steps * pages_per_step * page
    chunk = pages_per_step * page

    def copies(bb, ss, sl):
        out = []
        for p in range(pages_per_step):
            pg = pt_ref[bb, ss * pages_per_step + p]
            out.append(pltpu.make_async_copy(kc_ref.at[pg], kbuf.at[sl, :, pl.ds(p * page, page)],
                                             sem.at[sl, 0]))
            if fox:
                vdst = vbuf.at[sl, :, pl.ds(p * page, page)]
            else:
                vdst = vbuf.at[sl, pl.ds(p * page * n_heads_v, page * n_heads_v), :]
            out.append(pltpu.make_async_copy(vc_ref.at[pg], vdst, sem.at[sl, 1]))
        return out

    @pl.when(t == 0)
    def _():
        for cp in copies(b, s, slot):
            cp.start()

    @pl.when(t + 1 < nb * n_steps)
    def _():
        t1 = t + 1
        for cp in copies(t1 // n_steps, t1 % n_steps, t1 % 2):
            cp.start()

    row = lax.broadcasted_iota(jnp.int32, (n_rows, 1), 0)

    def bias(col0, width):
        cb = c_ref[0, :, pl.ds(col0, width)]
        return jnp.broadcast_to(cb[:, None, :], (n_groups, n_new, width)).reshape(n_rows, width)

    @pl.when(s == 0)
    def _():
        qt = jnp.concatenate([q_ref[...]] * n_groups, axis=0)
        colg = lax.broadcasted_iota(jnp.int32, (1, w), 1) // HEAD_DIM
        qbd_ref[...] = jnp.where(row // n_new == colg, qt, 0.0).astype(BF16)
        m_ref[...] = jnp.full(m_ref.shape, -jnp.inf, F32)
        l_ref[...] = jnp.zeros(l_ref.shape, F32)
        acc_ref[...] = jnp.zeros(acc_ref.shape, F32)
        if fox:
            lane = lax.broadcasted_iota(jnp.int32, (1, page), 1)
            cq_ref[...] = jnp.sum(jnp.where(lane == row % n_new, bias(n_past, page), 0.0),
                                  axis=1, keepdims=True)
        else:
            cq_ref[...] = jnp.zeros(cq_ref.shape, F32)

    def update(sc, pv):
        cq = cq_ref[...]
        m_prev = m_ref[...]
        m_new = jnp.maximum(m_prev, cq + jnp.max(sc, axis=1, keepdims=True))
        p = jnp.exp(sc + (cq - m_new))
        alpha = jnp.exp(m_prev - m_new)
        l_ref[...] = alpha * l_ref[...] + jnp.sum(p, axis=1, keepdims=True)
        acc_ref[...] = alpha * acc_ref[...] + pv(p.astype(BF16))
        m_ref[...] = m_new

    def pv_heads(p, v_of_head):
        rows_per_head = 2 * n_new
        return jnp.concatenate(
            [_nn(p[h * rows_per_head:(h + 1) * rows_per_head], v_of_head(h)) for h in range(n_heads_v)],
            axis=0)

    for cp in copies(b, s, slot):
        cp.wait()

    sc = _nn(qbd_ref[...], kbuf[slot].astype(BF16))
    if fox:
        sc = sc - bias(pl.multiple_of(s * chunk, chunk), chunk)
        vt = vbuf[slot].astype(BF16)
        update(sc, lambda p: _nt(p, vt))
    else:
        update(sc, lambda p: pv_heads(
            p, lambda h: vbuf[slot, pl.ds(h, chunk, stride=n_heads_v), :].astype(BF16)))

    @pl.when(s == n_steps - 1)
    def _():
        pad = jnp.zeros((page - n_new, w), F32)
        kn = jnp.concatenate([kn_ref[...], pad], axis=0).astype(BF16)
        vn = jnp.concatenate([vn_ref[...], pad], axis=0).astype(BF16)
        sn = _nt(qbd_ref[...], kn)
        if fox:
            sn = sn - bias(n_past, page)
        lane = lax.broadcasted_iota(jnp.int32, (1, page), 1)
        sn = jnp.where(lane <= row % n_new, sn, NEG_INF)
        if fox:
            update(sn, lambda p: _nn(p, vn))
        else:
            update(sn, lambda p: pv_heads(p, lambda h: vn[:, h * LANES:(h + 1) * LANES]))

        on = acc_ref[...] / l_ref[...]
        if fox:
            col = lax.broadcasted_iota(jnp.int32, (1, 1, w), 2)
            grp = lax.broadcasted_iota(jnp.int32, (n_groups, 1, 1), 0)
            o3 = on.reshape(n_groups, n_new, w)
            o_ref[...] = jnp.sum(jnp.where(grp == col // HEAD_DIM, o3, 0.0), axis=0)
        else:
            lam = _lambda(lq1[...], lk1[...], lq2[...], lk2[...], lam0)
            for h in range(n_heads_v):
                r0 = 2 * h * n_new
                oh = on[r0:r0 + n_new] - lam * on[r0 + n_new:r0 + 2 * n_new]
                o_ref[:, h * LANES:(h + 1) * LANES] = _rms(oh, g_ref[...]) * (1.0 - lam0)


def sample_attention(page_table, q, k_new, v_new, k_cache_t, v_cache, pages_per_step, *,
                     c_all=None, lams=None, gain=None, lam0=0.0):
    fox = c_all is not None
    nb, n_pages = page_table.shape
    _, w, page = k_cache_t.shape
    n_new = q.shape[0] // nb
    n_steps = n_pages // pages_per_step
    chunk = pages_per_step * page
    n_rows = (w // HEAD_DIM) * n_new
    new_spec = pl.BlockSpec((n_new, w), lambda b, s, pt: (b, 0))
    any_spec = pl.BlockSpec(memory_space=pl.ANY)
    if fox:
        ins = [q, k_new, v_new, c_all, k_cache_t, v_cache]
        in_specs = [new_spec, new_spec, new_spec,
                    pl.BlockSpec((1,) + c_all.shape[1:], lambda b, s, pt: (b, 0, 0)),
                    any_spec, any_spec]
        v_scratch = pltpu.VMEM((2, w, chunk), F32)
        acc_cols = w
    else:
        vec = pl.BlockSpec((1, HEAD_DIM), lambda b, s, pt: (0, 0))
        ins = ([q, k_new, v_new] + [x.reshape(1, HEAD_DIM) for x in lams]
               + [gain.reshape(1, LANES), k_cache_t, v_cache])
        in_specs = [new_spec, new_spec, new_spec, vec, vec, vec, vec,
                    pl.BlockSpec((1, LANES), lambda b, s, pt: (0, 0)), any_spec, any_spec]
        v_scratch = pltpu.VMEM((2, chunk * (w // LANES), LANES), F32)
        acc_cols = LANES
    grid_spec = pltpu.PrefetchScalarGridSpec(
        num_scalar_prefetch=1, grid=(nb, n_steps), in_specs=in_specs,
        out_specs=new_spec,
        scratch_shapes=[pltpu.VMEM((2, w, chunk), F32), v_scratch,
                        pltpu.SemaphoreType.DMA((2, 2)),
                        pltpu.VMEM((n_rows, w), BF16),
                        pltpu.VMEM((n_rows, 1), F32), pltpu.VMEM((n_rows, 1), F32),
                        pltpu.VMEM((n_rows, acc_cols), F32), pltpu.VMEM((n_rows, 1), F32)])
    nbytes = 4 * chunk * w * 4 + 2 * chunk * w * 2 + 4 * n_rows * chunk * 4
    if fox:
        nbytes += 2 * c_all.shape[1] * c_all.shape[2] * 4
    return pl.pallas_call(
        functools.partial(_sample_attn_kernel, fox=fox, pages_per_step=pages_per_step, n_steps=n_steps,
                          page=page, n_new=n_new, lam0=lam0),
        grid_spec=grid_spec,
        out_shape=jax.ShapeDtypeStruct((nb * n_new, w), F32),
        compiler_params=_params(2, nbytes),
        name="fox_sample" if fox else "diff_sample",
    )(page_table, *ins)


def _merge_kernel(fo_ref, do_ref, wf_ref, wd_ref, gf_ref, gd_ref, o_ref, wfb, wdb):
    @pl.when(pl.program_id(1) == 0)
    def _():
        wfb[...] = wf_ref[0].astype(BF16)
        wdb[...] = wd_ref[0].astype(BF16)

    pf = _nn(fo_ref[...].astype(BF16), wfb[...])
    pd = _nn(do_ref[...].astype(BF16), wdb[...])
    o_ref[...] = (jax.nn.sigmoid(gf_ref[...]) * pf + jax.nn.sigmoid(gd_ref[...]) * pd).astype(o_ref.dtype)


def branch_merge(fo, do, w_branch, gates, tm, tn):
    t, bw = fo.shape
    d = w_branch.shape[2]
    nn = d // tn
    nbytes = 4 * tm * bw * fo.dtype.itemsize + 4 * bw * tn * 4 + 2 * bw * tn * 2 + 6 * tm * tn * 4
    return pl.pallas_call(
        _merge_kernel, grid=(nn, t // tm),
        in_specs=[pl.BlockSpec((tm, bw), lambda j, i: (i, 0)),
                  pl.BlockSpec((tm, bw), lambda j, i: (i, 0)),
                  pl.BlockSpec((1, bw, tn), lambda j, i: (0, 0, j)),
                  pl.BlockSpec((1, bw, tn), lambda j, i: (1, 0, j)),
                  pl.BlockSpec((tm, tn), lambda j, i: (i, j)),
                  pl.BlockSpec((tm, tn), lambda j, i: (i, nn + j))],
        out_specs=pl.BlockSpec((tm, tn), lambda j, i: (i, j)),
        out_shape=jax.ShapeDtypeStruct((t, d), BF16),
        scratch_shapes=[pltpu.VMEM((bw, tn), BF16), pltpu.VMEM((bw, tn), BF16)],
        compiler_params=_params(2, nbytes),
        name="branch_merge",
    )(fo, do, w_branch, w_branch, gates, gates)


def _gelu_tanh(x):
    return 0.5 * x * (1.0 + jnp.tanh(math.sqrt(2.0 / math.pi) * (x + 0.044715 * (x * x * x))))


def _causal_conv(u, cw, prev1, prev2, period):
    n = u.shape[0]
    pos = lax.broadcasted_iota(jnp.int32, (n, 1), 0) % period
    u1 = jnp.where(pos == 0, prev1, pltpu.roll(u, 1, 0))
    u2 = jnp.where(pos == 0, prev2, jnp.where(pos == 1, prev1, pltpu.roll(u, 2, 0)))
    return cw[0:1] * u2 + cw[1:2] * u1 + cw[2:3] * u


def _ffn_prompt_kernel(h_ref, wg_ref, wv_ref, cwg_ref, cwv_ref, wd_ref, f_ref, cs_ref,
                       carry_ref, *, tiles_per_seq, sub):
    i = pl.program_id(0)
    f = pl.program_id(1)
    tm = h_ref.shape[0]
    tf = wg_ref.shape[1]
    h = h_ref[...]

    @pl.when(i % tiles_per_seq == 0)
    def _():
        carry_ref[f] = jnp.zeros(carry_ref.shape[1:], F32)

    @pl.when(f == 0)
    def _():
        f_ref[...] = jnp.zeros(f_ref.shape, F32)

    acts = []
    for c in range(tf // sub):
        cs = slice(c * sub, (c + 1) * sub)
        ug = _nn(h, wg_ref[:, cs])
        uv = _nn(h, wv_ref[:, cs])
        pg = carry_ref[f, 0, :, cs]
        pv = carry_ref[f, 1, :, cs]
        yg = _causal_conv(ug, cwg_ref[:, cs], pg[SUBLANES - 1:SUBLANES], pg[SUBLANES - 2:SUBLANES - 1], tm)
        yv = _causal_conv(uv, cwv_ref[:, cs], pv[SUBLANES - 1:SUBLANES], pv[SUBLANES - 2:SUBLANES - 1], tm)
        carry_ref[f, 0, :, cs] = ug[tm - SUBLANES:tm]
        carry_ref[f, 1, :, cs] = uv[tm - SUBLANES:tm]
        cs_ref[0, 0, :, cs] = ug[tm - (CONV_WIDTH - 1):tm]
        cs_ref[0, 1, :, cs] = uv[tm - (CONV_WIDTH - 1):tm]
        acts.append((_gelu_tanh(yg) * yv).astype(BF16))

    f_ref[...] += _nn(jnp.concatenate(acts, axis=1), wd_ref[...])


def cast_bf16(w):
    n, m = w.shape
    rows = _tile(n, max(SUBLANES, (4 * 1024 * 1024) // (4 * m)))

    def kern(w_ref, o_ref):
        o_ref[...] = w_ref[...].astype(BF16)

    return pl.pallas_call(
        kern, grid=(n // rows,),
        in_specs=[pl.BlockSpec((rows, m), lambda i: (i, 0))],
        out_specs=pl.BlockSpec((rows, m), lambda i: (i, 0)),
        out_shape=jax.ShapeDtypeStruct((n, m), BF16),
        compiler_params=_params(1, 2 * rows * m * 6),
        name="cast_bf16",
    )(w)


def ffn_prompt(h, w_up, conv_w, w_down, seq, tm, tf, sub):
    t, d = h.shape
    dff = w_down.shape[0]
    nf = dff // tf
    tiles_per_seq = seq // tm
    nbytes = (2 * tm * d * 2 + 4 * d * tf * 2 + 2 * tf * d * 2 + 2 * tm * d * 4
              + 10 * tm * tf * 4 + nf * 2 * SUBLANES * tf * 4)
    return pl.pallas_call(
        functools.partial(_ffn_prompt_kernel, tiles_per_seq=tiles_per_seq, sub=sub),
        grid=(t // tm, nf),
        in_specs=[pl.BlockSpec((tm, d), lambda i, f: (i, 0)),
                  pl.BlockSpec((d, tf), lambda i, f: (0, f)),
                  pl.BlockSpec((d, tf), lambda i, f: (0, nf + f)),
                  pl.BlockSpec((CONV_WIDTH, tf), lambda i, f: (0, f)),
                  pl.BlockSpec((CONV_WIDTH, tf), lambda i, f: (0, nf + f)),
                  pl.BlockSpec((tf, d), lambda i, f: (f, 0))],
        out_specs=[pl.BlockSpec((tm, d), lambda i, f: (i, 0)),
                   pl.BlockSpec((1, 2, CONV_WIDTH - 1, tf), lambda i, f: (i, 0, 0, f))],
        out_shape=[jax.ShapeDtypeStruct((t, d), F32),
                   jax.ShapeDtypeStruct((t // tm, 2, CONV_WIDTH - 1, dff), F32)],
        scratch_shapes=[pltpu.VMEM((nf, 2, SUBLANES, tf), F32)],
        compiler_params=_params(2, nbytes),
        name="ffn_prompt",
    )(h, w_up, w_up, conv_w, conv_w, w_down)


def _ffn_sample_kernel(h_ref, wg_ref, wv_ref, cwg_ref, cwv_ref, wd_ref, sg_ref, sv_ref, f_ref, cs_ref,
                       *, n_new):
    f = pl.program_id(0)
    t = h_ref.shape[0]
    nb = t // n_new
    tf = wg_ref.shape[1]
    h = h_ref[...]
    ug = _nn(h, wg_ref[...])
    uv = _nn(h, wv_ref[...])

    def rows_of(state, r):
        return jnp.broadcast_to(state[:, r:r + 1, :], (nb, n_new, tf)).reshape(t, tf)

    sg = sg_ref[...]
    sv = sv_ref[...]
    yg = _causal_conv(ug, cwg_ref[...], rows_of(sg, 1), rows_of(sg, 0), n_new)
    yv = _causal_conv(uv, cwv_ref[...], rows_of(sv, 1), rows_of(sv, 0), n_new)
    keep = CONV_WIDTH - 1
    cs_ref[0] = ug.reshape(nb, n_new, tf)[:, n_new - keep:, :]
    cs_ref[1] = uv.reshape(nb, n_new, tf)[:, n_new - keep:, :]

    act = (_gelu_tanh(yg) * yv).astype(BF16)

    @pl.when(f == 0)
    def _():
        f_ref[...] = jnp.zeros(f_ref.shape, F32)

    f_ref[...] += _nn(act, wd_ref[...])


def ffn_sample(h, w_up, conv_w, w_down, state, n_new, tf):
    t, d = h.shape
    dff = w_down.shape[0]
    nf = dff // tf
    nb = t // n_new
    keep = CONV_WIDTH - 1
    nbytes = (2 * t * d * 2 + 4 * d * tf * 2 + 2 * tf * d * 2 + 2 * t * d * 4
              + 12 * t * tf * 4 + 8 * nb * SUBLANES * tf * 4)
    return pl.pallas_call(
        functools.partial(_ffn_sample_kernel, n_new=n_new),
        grid=(nf,),
        in_specs=[pl.BlockSpec((t, d), lambda f: (0, 0)),
                  pl.BlockSpec((d, tf), lambda f: (0, f)),
                  pl.BlockSpec((d, tf), lambda f: (0, nf + f)),
                  pl.BlockSpec((CONV_WIDTH, tf), lambda f: (0, f)),
                  pl.BlockSpec((CONV_WIDTH, tf), lambda f: (0, nf + f)),
                  pl.BlockSpec((tf, d), lambda f: (f, 0)),
                  pl.BlockSpec((nb, keep, tf), lambda f: (0, 0, f)),
                  pl.BlockSpec((nb, keep, tf), lambda f: (0, 0, nf + f))],
        out_specs=[pl.BlockSpec((t, d), lambda f: (0, 0)),
                   pl.BlockSpec((2, nb, keep, tf), lambda f: (0, 0, 0, f))],
        out_shape=[jax.ShapeDtypeStruct((t, d), F32),
                   jax.ShapeDtypeStruct((2, nb, keep, dff), F32)],
        compiler_params=_params(1, nbytes),
        name="ffn_sample",
    )(h, w_up, w_up, conv_w, conv_w, w_down, state, state)


def _lambda_init(layer_idx):
    return 0.8 - 0.6 * math.exp(-0.3 * layer_idx)


def _rope_angles(pos):
    half = HEAD_DIM // 2
    inv_freq = ROPE_THETA ** (-jnp.arange(half, dtype=F32) / half)
    ang = pos.astype(F32)[:, None] * inv_freq[None, :]
    return jnp.cos(ang), jnp.sin(ang)


def _rope_tables(pos):
    cos, sin = _rope_angles(pos)
    return (jnp.concatenate([cos, cos, cos, cos], axis=1),
            jnp.concatenate([-sin, sin, -sin, sin], axis=1))


def _mixer_tail(x, fo, do, gates, w_branch_l, w_out_l, g_post, g_ffn_pre, tm, tn):
    t, d = x.shape
    merged = branch_merge(fo, do, w_branch_l, gates, tm, tn)
    mix = matmul(merged, w_out_l, w_transposed=False, col_block=0, n_blocks=d // tn, tn=tn, tm=tm,
                 outs=[_token_out(t, d, tm, tn, F32)], epilogue=_epi_plain(1.0))[0]
    return residual_norm(x, mix, g_post, g_ffn_pre, _tile(t, 512))


def kernel(x_prompt, x_sample, cache_fox_k, cache_fox_v, cache_fox_logf, cache_diff_k, cache_diff_v,
           state_ffn_conv, page_table, norm_mix_pre, norm_mix_post, w_in, b_forget, lam_q1, lam_k1,
           lam_q2, lam_k2, diff_subln, w_branch, w_out, norm_ffn_pre, norm_ffn_post, w_up, conv_ffn,
           w_down):
    nbp, seq, d = x_prompt.shape
    nbs, n_new, _ = x_sample.shape
    depth, n_pool, page, fox_heads, _ = cache_fox_k.shape
    diff_heads = cache_diff_v.shape[3]
    fw = fox_heads * HEAD_DIM
    dqk = 2 * diff_heads * HEAD_DIM
    dvw = diff_heads * 2 * HEAD_DIM
    dff = w_down.shape[1]
    n_pages = page_table.shape[1]
    n_past = n_pages * page
    tp, ts = nbp * seq, nbs * n_new

    tm = _tile(seq, ROW_TILE)
    tq = _tile(seq, ATTN_TILE)
    tf = _tile(dff, FF_TILE)
    tn = _tile(fw, COL_TILE)
    tn_d = _tile(d, COL_TILE)
    pages_per_step = _tile(n_pages, PAGES_PER_STEP)
    assert dqk == fw and dvw == fw and fw % tn == 0 and (2 * d) % tn == 0

    tps = seq // tm
    cos_p, sin_p = _rope_tables(jnp.arange(seq))
    cos_pt, sin_pt = [x.T for x in _rope_angles(jnp.arange(seq))]
    cos_s, sin_s = _rope_tables(n_past + jnp.tile(jnp.arange(n_new), nbs))

    def rope_extras(cos, sin, rows, period):
        spec = pl.BlockSpec((rows, LANES), lambda j, i: (i % period, 0))
        return [(cos, spec), (sin, spec)]

    xp = x_prompt.reshape(tp, d)
    xs = x_sample.reshape(ts, d)
    outs_p = [[] for _ in range(6)]
    outs_s = [[] for _ in range(6)]
    nblk = fw // tn
    for l in range(depth):
        lam0 = _lambda_init(l)
        lams = (lam_q1[l], lam_k1[l], lam_q2[l], lam_k2[l])
        w_t = w_in[l].T
        w_ff_t = w_t[3 * fw:3 * fw + fox_heads]
        w_rest_t = w_t[3 * fw + fox_heads:]
        proj = functools.partial(matmul, w_transposed=True, n_blocks=nblk, tn=tn)

        hp = rmsnorm_bf16(xp, norm_mix_pre[l], tm)
        pj = functools.partial(proj, hp, tm=tm)
        fm_pair = [_feature_out(nbp, fw, seq, tm, tn, F32), _feature_out(nbp, fw, seq, tm, tn, BF16)]
        fq = pj(w_t, col_block=0, outs=[_token_out(tp, fw, tm, tn, BF16)], epilogue=_epi_plain(Q_SCALE))[0]
        fk, fkb = pj(w_t, col_block=nblk, outs=fm_pair, epilogue=_epi_plain(1.0), feature_major=True)
        fv, fvb = pj(w_t, col_block=2 * nblk, outs=fm_pair, epilogue=_epi_plain(1.0), feature_major=True)
        logf = forget_gate(hp, w_ff_t, b_forget[l], seq, tm)
        dq = pj(w_rest_t, col_block=0, outs=[_token_out(tp, dqk, tm, tn, BF16)],
                epilogue=_epi_rope(Q_SCALE), extras=rope_extras(cos_p, sin_p, tm, tps))[0]
        fm_spec = pl.BlockSpec((HEAD_DIM // 2, tm), lambda j, i: (0, i % tps))
        dk, dkb = pj(w_rest_t, col_block=nblk, outs=fm_pair, epilogue=_epi_rope_fm,
                     extras=[(cos_pt, fm_spec), (sin_pt, fm_spec)], feature_major=True)
        dv, dvb = pj(w_rest_t, col_block=2 * nblk,
                     outs=[(jax.ShapeDtypeStruct((tp, dvw // LANES, LANES), F32),
                            pl.BlockSpec((tm, tn // LANES, LANES), lambda j, i: (i, j, 0))),
                           _token_out(tp, dvw, tm, tn, BF16)],
                     epilogue=_epi_heads)
        gates = matmul(hp, w_rest_t, w_transposed=True, col_block=3 * nblk, n_blocks=2 * d // tn, tn=tn,
                       tm=tm, outs=[_token_out(tp, 2 * d, tm, tn, F32)], epilogue=_epi_plain(1.0))[0]
        c_t, c = cumsum_prompt(logf, tq)
        fo = fox_prompt(fq, fkb, fvb, c, c_t, tq)
        do = diff_prompt(dq, dkb, dvb, lams, diff_subln[l], lam0, tq)
        x2, h2 = _mixer_tail(xp, fo, do, gates, w_branch[l], w_out[l], norm_mix_post[l], norm_ffn_pre[l],
                             tm, tn_d)
        w_up_b = cast_bf16(w_up[l])
        w_down_b = cast_bf16(w_down[l])
        f, conv_p = ffn_prompt(h2, w_up_b, conv_ffn[l], w_down_b, seq, tm, tf, _tile(tf, FF_SUB))
        xp = residual_norm(x2, f, norm_ffn_post[l], None, _tile(tp, 512))
        conv_p = conv_p[tps - 1::tps]
        conv_p = jnp.transpose(conv_p, (0, 2, 1, 3)).reshape(nbp, CONV_WIDTH - 1, 2 * dff)

        def heads_last(x_t, n_heads):
            return jnp.transpose(x_t.reshape(nbp, n_heads, HEAD_DIM, seq), (0, 3, 1, 2))

        for lst, val in zip(outs_p, (heads_last(fk, fox_heads), heads_last(fv, fox_heads),
                                     jnp.transpose(logf, (0, 2, 1)),
                                     heads_last(dk, 2 * diff_heads),
                                     dv.reshape(nbp, seq, diff_heads, 2 * HEAD_DIM), conv_p)):
            lst.append(val)

        hs = rmsnorm_bf16(xs, norm_mix_pre[l], ts)
        pj = functools.partial(proj, hs, tm=ts)
        tok = lambda n: [_token_out(ts, n, ts, tn, F32)]
        fq = pj(w_t, col_block=0, outs=tok(fw), epilogue=_epi_plain(Q_SCALE))[0]
        fk = pj(w_t, col_block=nblk, outs=tok(fw), epilogue=_epi_plain(1.0))[0]
        fv = pj(w_t, col_block=2 * nblk, outs=tok(fw), epilogue=_epi_plain(1.0))[0]
        logf = forget_gate(hs, w_ff_t, b_forget[l], ts, ts)[0]
        dq = pj(w_rest_t, col_block=0, outs=tok(dqk), epilogue=_epi_rope(Q_SCALE),
                extras=rope_extras(cos_s, sin_s, ts, 1))[0]
        dk = pj(w_rest_t, col_block=nblk, outs=tok(dqk), epilogue=_epi_rope(1.0),
                extras=rope_extras(cos_s, sin_s, ts, 1))[0]
        dv = pj(w_rest_t, col_block=2 * nblk, outs=tok(dvw), epilogue=_epi_plain(1.0))[0]
        gates = matmul(hs, w_rest_t, w_transposed=True, col_block=3 * nblk, n_blocks=2 * d // tn, tn=tn,
                       tm=ts, outs=[_token_out(ts, 2 * d, ts, tn, F32)], epilogue=_epi_plain(1.0))[0]
        feature_major = lambda cache: jnp.transpose(cache, (0, 2, 3, 1)).reshape(n_pool, -1, page)
        c_all = cumsum_sample(page_table, logf, jnp.transpose(cache_fox_logf[l], (0, 2, 1)), n_new)
        fo = sample_attention(page_table, fq, fk, fv, feature_major(cache_fox_k[l]),
                              feature_major(cache_fox_v[l]), pages_per_step, c_all=c_all)
        do = sample_attention(page_table, dq, dk, dv, feature_major(cache_diff_k[l]),
                              cache_diff_v[l].reshape(n_pool, page * diff_heads, 2 * HEAD_DIM),
                              pages_per_step, lams=lams, gain=diff_subln[l], lam0=lam0)
        x2, h2 = _mixer_tail(xs, fo, do, gates, w_branch[l], w_out[l], norm_mix_post[l], norm_ffn_pre[l],
                             ts, tn_d)
        f, conv_s = ffn_sample(h2, w_up_b, conv_ffn[l], w_down_b, state_ffn_conv[l], n_new, tf)
        xs = residual_norm(x2, f, norm_ffn_post[l], None, ts)
        conv_s = jnp.transpose(conv_s, (1, 2, 0, 3)).reshape(nbs, CONV_WIDTH - 1, 2 * dff)
        for lst, val in zip(outs_s, (fk.reshape(nbs, n_new, fox_heads, HEAD_DIM),
                                     fv.reshape(nbs, n_new, fox_heads, HEAD_DIM),
                                     logf.T.reshape(nbs, n_new, fox_heads),
                                     dk.reshape(nbs, n_new, 2 * diff_heads, HEAD_DIM),
                                     dv.reshape(nbs, n_new, diff_heads, 2 * HEAD_DIM), conv_s)):
            lst.append(val)

    return (xp.reshape(nbp, seq, d), xs.reshape(nbs, n_new, d),
            *[jnp.stack(v) for v in outs_p], *[jnp.stack(v) for v in outs_s])
```

```python
import functools
import math

import jax
import jax.numpy as jnp
from jax import lax
from jax.experimental import pallas as pl
from jax.experimental.pallas import tpu as pltpu

HEAD_DIM = 64
CONV_WIDTH = 3
ROPE_THETA = 10000.0
RMS_EPS = 1e-6
NEG_INF = -1e30
Q_SCALE = HEAD_DIM ** -0.5
LOG2E = math.log2(math.e)

LANES = 128
SUBLANES = 8
V7X_VMEM_BYTES = 64 * 1024 * 1024
VMEM_CAP_BYTES = V7X_VMEM_BYTES - 8 * 1024 * 1024

F32 = jnp.float32
BF16 = jnp.bfloat16

ROW_TILE = 1024
COL_TILE = 1024
ATTN_TILE = 512
FF_TILE = 1024
FF_SUB = 256
MM_SUB = 256
PAGES_PER_STEP = 8


def _vmem(nbytes):
    return int(min(VMEM_CAP_BYTES, max(32 * 1024 * 1024, 2 * nbytes)))


def _params(n_axes, vmem_bytes):
    return pltpu.CompilerParams(dimension_semantics=("arbitrary",) * n_axes,
                                vmem_limit_bytes=_vmem(vmem_bytes))


def _nt(a, b):
    return lax.dot_general(a, b, (((1,), (1,)), ((), ())), preferred_element_type=F32)


def _nn(a, b):
    return jnp.dot(a, b, preferred_element_type=F32)


def _rms(x, g):
    return x * lax.rsqrt(jnp.mean(x * x, axis=-1, keepdims=True) + RMS_EPS) * g


def _tile(n, pref):
    t = min(n, pref)
    while n % t:
        t //= 2
    return t


def _rmsnorm_kernel(x_ref, g_ref, o_ref):
    o_ref[...] = _rms(x_ref[...], g_ref[...]).astype(o_ref.dtype)


def rmsnorm_bf16(x, g, tm):
    t, d = x.shape
    return pl.pallas_call(
        _rmsnorm_kernel,
        grid=(t // tm,),
        in_specs=[pl.BlockSpec((tm, d), lambda i: (i, 0)),
                  pl.BlockSpec((1, d), lambda i: (0, 0))],
        out_specs=pl.BlockSpec((tm, d), lambda i: (i, 0)),
        out_shape=jax.ShapeDtypeStruct((t, d), BF16),
        compiler_params=_params(1, 2 * tm * d * 6),
        name="rmsnorm_bf16",
    )(x, g.reshape(1, d))


def _resnorm_kernel(x_ref, z_ref, g1_ref, o_ref):
    o_ref[...] = x_ref[...] + _rms(z_ref[...], g1_ref[...])


def residual_norm(x, z, g1, tm):
    t, d = x.shape
    row = pl.BlockSpec((tm, d), lambda i: (i, 0))
    vec = pl.BlockSpec((1, d), lambda i: (0, 0))
    return pl.pallas_call(
        _resnorm_kernel, grid=(t // tm,), in_specs=[row, row, vec], out_specs=row,
        out_shape=jax.ShapeDtypeStruct((t, d), F32),
        compiler_params=_params(1, 2 * tm * d * 12), name="residual_norm",
    )(x, z, g1.reshape(1, d))


def _mm_kernel(*refs, n_extra, n_out, epilogue, w_transposed, feature_major):
    a_ref, w_ref = refs[0], refs[1]
    extra = refs[2:2 + n_extra]
    outs = refs[2 + n_extra:2 + n_extra + n_out]
    wb_ref = refs[-1]

    @pl.when(pl.program_id(1) == 0)
    def _():
        wb_ref[...] = w_ref[...].astype(BF16)

    a = a_ref[...].astype(BF16)
    tn = wb_ref.shape[0] if w_transposed else wb_ref.shape[1]
    sub = _tile(tn, MM_SUB)
    for c0 in range(0, tn, sub):
        if feature_major:
            acc = _nt(wb_ref[c0:c0 + sub, :], a)
        elif w_transposed:
            acc = _nt(a, wb_ref[c0:c0 + sub, :])
        else:
            acc = _nn(a, wb_ref[:, c0:c0 + sub])
        epilogue(acc, extra, outs, c0)


def _epi_plain(scale, feature_major=False):
    def epi(acc, extra, outs, c0):
        val = acc if scale == 1.0 else acc * scale
        for o in outs:
            if feature_major:
                o[0, c0:c0 + acc.shape[0], :] = val.astype(o.dtype)
            else:
                o[:, c0:c0 + acc.shape[1]] = val.astype(o.dtype)
    return epi


def _epi_heads(acc, extra, outs, c0):
    native, dense = outs
    for h in range(acc.shape[1] // LANES):
        native[:, c0 // LANES + h, :] = acc[:, h * LANES:(h + 1) * LANES]
    dense[:, c0:c0 + acc.shape[1]] = acc.astype(dense.dtype)


def _epi_rope(scale):
    def epi(acc, extra, outs, c0):
        cos = extra[0][...]
        sin = extra[1][...]
        first_half = (lax.broadcasted_iota(jnp.int32, (1, LANES), 1) % HEAD_DIM) < HEAD_DIM // 2
        for c in range(acc.shape[1] // LANES):
            x = acc[:, c * LANES:(c + 1) * LANES]
            swapped = jnp.where(first_half, pltpu.roll(x, LANES - HEAD_DIM // 2, 1),
                                pltpu.roll(x, HEAD_DIM // 2, 1))
            val = x * cos + swapped * sin
            if scale != 1.0:
                val = val * scale
            for o in outs:
                o[:, c0 + c * LANES:c0 + (c + 1) * LANES] = val.astype(o.dtype)
    return epi


def _epi_rope_fm(acc, extra, outs, c0):
    cos = extra[0][...]
    sin = extra[1][...]
    half = HEAD_DIM // 2
    for m in range(acc.shape[0] // HEAD_DIM):
        x1 = acc[m * HEAD_DIM:m * HEAD_DIM + half]
        x2 = acc[m * HEAD_DIM + half:(m + 1) * HEAD_DIM]
        y1 = x1 * cos - x2 * sin
        y2 = x2 * cos + x1 * sin
        r0 = c0 + m * HEAD_DIM
        for o in outs:
            o[0, r0:r0 + half, :] = y1.astype(o.dtype)
            o[0, r0 + half:r0 + HEAD_DIM, :] = y2.astype(o.dtype)


def matmul(a, w, *, w_transposed, col_block, n_blocks, tn, tm, outs, epilogue, extras=(),
           feature_major=False):
    t, k = a.shape
    in_specs = [pl.BlockSpec((tm, k), lambda j, i: (i, 0))]
    if w_transposed:
        in_specs.append(pl.BlockSpec((tn, k), lambda j, i: (col_block + j, 0)))
        w_scratch = pltpu.VMEM((tn, k), BF16)
    else:
        in_specs.append(pl.BlockSpec((k, tn), lambda j, i: (0, col_block + j)))
        w_scratch = pltpu.VMEM((k, tn), BF16)
    in_specs += [spec for _, spec in extras]
    nbytes = 2 * tm * k * a.dtype.itemsize + 2 * k * tn * 4 + k * tn * 2 + 2 * tm * tn * 4
    nbytes += sum(2 * tm * tn * s.dtype.itemsize for s, _ in outs)
    return pl.pallas_call(
        functools.partial(_mm_kernel, n_extra=len(extras), n_out=len(outs), epilogue=epilogue,
                          w_transposed=w_transposed, feature_major=feature_major),
        grid=(n_blocks, t // tm),
        in_specs=in_specs,
        out_specs=[spec for _, spec in outs],
        out_shape=[s for s, _ in outs],
        scratch_shapes=[w_scratch],
        compiler_params=_params(2, nbytes),
        name="proj_matmul",
    )(a, w, *[x for x, _ in extras])


def _token_out(t, n, tm, tn, dtype):
    return jax.ShapeDtypeStruct((t, n), dtype), pl.BlockSpec((tm, tn), lambda j, i: (i, j))


def _feature_out(nb, n, seq, tm, tn, dtype):
    tps = seq // tm
    return (jax.ShapeDtypeStruct((nb, n, seq), dtype),
            pl.BlockSpec((1, tn, tm), lambda j, i: (i // tps, j, i % tps)))


def _log_sigmoid(x):
    return jnp.minimum(x, 0.0) - jnp.log1p(jnp.exp(-jnp.abs(x)))


def _logf_kernel(a_ref, w_ref, b_ref, o_ref):
    acc = _nt(w_ref[...].astype(BF16), a_ref[...])
    o_ref[0] = _log_sigmoid(acc + b_ref[...])


def forget_gate(a, w_ff_t, b_forget, seq, tm):
    t, k = a.shape
    nh = w_ff_t.shape[0]
    tps = seq // tm
    return pl.pallas_call(
        _logf_kernel, grid=(t // tm,),
        in_specs=[pl.BlockSpec((tm, k), lambda i: (i, 0)),
                  pl.BlockSpec((nh, k), lambda i: (0, 0)),
                  pl.BlockSpec((nh, 1), lambda i: (0, 0))],
        out_specs=pl.BlockSpec((1, nh, tm), lambda i: (i // tps, 0, i % tps)),
        out_shape=jax.ShapeDtypeStruct((t // seq, nh, seq), F32),
        compiler_params=_params(1, 2 * tm * k * 2 + k * LANES * 8),
        name="forget_gate",
    )(a, w_ff_t, b_forget.reshape(nh, 1))


def _scan_lanes(x):
    n = x.shape[1]
    lane = lax.broadcasted_iota(jnp.int32, (1, n), 1)
    shift = 1
    while shift < n:
        x = x + jnp.where(lane >= shift, pltpu.roll(x, shift, 1), 0.0)
        shift *= 2
    return x


def _split3(c):
    hi = c.astype(BF16)
    r1 = c - hi.astype(F32)
    mid = r1.astype(BF16)
    lo = (r1 - mid.astype(F32)).astype(BF16)
    return hi, mid, lo


def _cumsum_prompt_kernel(lf_ref, ct_ref, c_ref, *, blk):
    c = _scan_lanes(lf_ref[0]) * LOG2E
    ct_ref[0] = c
    eye = (lax.broadcasted_iota(jnp.int32, (blk, blk), 0)
           == lax.broadcasted_iota(jnp.int32, (blk, blk), 1)).astype(BF16)
    for j in range(c.shape[1] // blk):
        hi, mid, lo = _split3(c[:, j * blk:(j + 1) * blk])
        c_ref[j * blk:(j + 1) * blk, :] = _nt(eye, hi) + _nt(eye, mid) + _nt(eye, lo)


def cumsum_prompt(logf_t, blk):
    b, nh, s = logf_t.shape
    return pl.pallas_call(
        functools.partial(_cumsum_prompt_kernel, blk=blk), grid=(b,),
        in_specs=[pl.BlockSpec((1, nh, s), lambda i: (i, 0, 0))],
        out_specs=[pl.BlockSpec((1, nh, s), lambda i: (i, 0, 0)),
                   pl.BlockSpec((s, nh), lambda i: (i, 0))],
        out_shape=[jax.ShapeDtypeStruct((b, nh, s), F32), jax.ShapeDtypeStruct((b * s, nh), F32)],
        compiler_params=_params(1, 8 * s * LANES * 4),
        name="cumsum_prompt",
    )(logf_t)


def _cumsum_sample_kernel(pt_ref, lfn_ref, cache_ref, c_ref, buf, sem, *, n_pages, page, n_new):
    b = pl.program_id(0)

    def page_copy(p):
        return pltpu.make_async_copy(cache_ref.at[pt_ref[b, p]], buf.at[:, pl.ds(p * page, page)], sem.at[0])

    for p in range(n_pages):
        page_copy(p).start()
    n_tok = lfn_ref.shape[1]
    tok = lax.broadcasted_iota(jnp.int32, (n_tok, 1), 0)
    j = lax.broadcasted_iota(jnp.int32, (1, page), 1)
    sel = ((tok == b * n_new + j) & (j < n_new)).astype(BF16)
    hi, mid, lo = _split3(lfn_ref[...])
    buf[:, n_pages * page:] = _nn(hi, sel) + _nn(mid, sel) + _nn(lo, sel)
    for p in range(n_pages):
        page_copy(p).wait()
    c_ref[0] = _scan_lanes(buf[...])


def cumsum_sample(page_table, logf_new_t, cache_logf_t, n_new):
    nb, n_pages = page_table.shape
    _, nh, page = cache_logf_t.shape
    n_cols = n_pages * page + page
    grid_spec = pltpu.PrefetchScalarGridSpec(
        num_scalar_prefetch=1, grid=(nb,),
        in_specs=[pl.BlockSpec(logf_new_t.shape, lambda i, pt: (0, 0)),
                  pl.BlockSpec(memory_space=pl.ANY)],
        out_specs=pl.BlockSpec((1, nh, n_cols), lambda i, pt: (i, 0, 0)),
        scratch_shapes=[pltpu.VMEM((nh, n_cols), F32), pltpu.SemaphoreType.DMA((1,))])
    return pl.pallas_call(
        functools.partial(_cumsum_sample_kernel, n_pages=n_pages, page=page, n_new=n_new),
        grid_spec=grid_spec,
        out_shape=jax.ShapeDtypeStruct((nb, nh, n_cols), F32),
        compiler_params=_params(1, 24 * nh * n_cols * 4),
        name="cumsum_sample",
    )(page_table, logf_new_t, cache_logf_t)


def _flash_pair(q2, kv_fn, i, tq, cqs, ck_fns, sum_rows):
    lane = lax.broadcasted_iota(jnp.int32, (1, LANES), 1)
    qms = [jnp.where(lane < HEAD_DIM, q2, jnp.zeros_like(q2)),
           jnp.where(lane >= HEAD_DIM, q2, jnp.zeros_like(q2))]

    rows = lax.broadcasted_iota(jnp.int32, (tq, tq), 0)
    cols = lax.broadcasted_iota(jnp.int32, (tq, tq), 1)

    def block(j, carry, diagonal):
        r0 = pl.multiple_of(j * tq, tq)
        kt, pvs = kv_fn(r0)
        out = []
        for a in range(2):
            m, l, acc = carry[a]
            s = _nn(qms[a], kt)
            if ck_fns is not None:
                s = s - ck_fns[a](r0)
            if diagonal:
                s = jnp.where(cols <= rows, s, NEG_INF)
            row_max = jnp.max(s, axis=1, keepdims=True)
            m_new = jnp.maximum(m, row_max if cqs is None else cqs[a] + row_max)
            p = jnp.exp2(s + ((-m_new) if cqs is None else (cqs[a] - m_new)))
            alpha = jnp.exp2(m - m_new)
            if sum_rows:
                l = alpha * l + jnp.sum(p, axis=1, keepdims=True)
            acc = alpha * acc + pvs[a](p.astype(BF16))
            out.append((m_new, l, acc))
        return tuple(out)

    init = (jnp.full((tq, 1), -jnp.inf, F32), jnp.zeros((tq, 1), F32), jnp.zeros((tq, LANES), F32))
    carry = lax.fori_loop(0, i, lambda j, c: block(j, c, False), (init, init))
    carry = block(i, carry, True)
    return [(acc, l) for _, l, acc in carry]


def _fox_prompt_kernel(q_ref, k_ref, v_ref, c_ref, ct_ref, o_ref, *, tq, n_heads):
    i = pl.program_id(1)
    lane = lax.broadcasted_iota(jnp.int32, (1, LANES), 1)
    head = lax.broadcasted_iota(jnp.int32, (1, n_heads), 1)
    feat = lax.broadcasted_iota(jnp.int32, (LANES, 1), 0)
    c_blk = c_ref[...]
    for hp in range(n_heads // 2):
        col = hp * LANES

        def kv_fn(r0, col=col):
            vt = v_ref[0, col:col + LANES, pl.ds(r0, tq)]
            vts = [jnp.where(feat < HEAD_DIM, vt, jnp.ones_like(vt)),
                   jnp.where(feat >= HEAD_DIM, vt, jnp.ones_like(vt))]
            return (k_ref[0, col:col + LANES, pl.ds(r0, tq)],
                    [lambda p, v=v: _nt(p, v) for v in vts])

        hs = (2 * hp, 2 * hp + 1)
        cqs = [jnp.sum(jnp.where(head == h, c_blk, 0.0), axis=1, keepdims=True) for h in hs]
        ck_fns = [lambda r0, h=h: ct_ref[0, h:h + 1, pl.ds(r0, tq)] for h in hs]
        (acc0, _), (acc1, _) = _flash_pair(q_ref[:, col:col + LANES], kv_fn, i, tq, cqs, ck_fns, False)
        o0 = acc0 / pltpu.roll(acc0, HEAD_DIM, 1)
        o1 = acc1 / pltpu.roll(acc1, HEAD_DIM, 1)
        o_ref[:, col:col + LANES] = jnp.where(lane < HEAD_DIM, o0, o1).astype(o_ref.dtype)


def _lambda(lq1, lk1, lq2, lk2, lam0):
    return (jnp.exp(jnp.sum(lq1 * lk1, axis=1, keepdims=True))
            - jnp.exp(jnp.sum(lq2 * lk2, axis=1, keepdims=True)) + lam0)


def _diff_prompt_kernel(q_ref, k_ref, v_ref, lq1, lk1, lq2, lk2, g_ref, o_ref, *, tq, n_heads, lam0):
    i = pl.program_id(1)
    lam = _lambda(lq1[...], lk1[...], lq2[...], lk2[...], lam0)
    for h in range(n_heads):
        col = h * LANES

        def kv_fn(r0, col=col):
            vb = v_ref[pl.ds(r0, tq), col:col + LANES]
            return k_ref[0, col:col + LANES, pl.ds(r0, tq)], [lambda p: _nn(p, vb)] * 2

        (acc0, l0), (acc1, l1) = _flash_pair(q_ref[:, col:col + LANES], kv_fn, i, tq, None, None, True)
        o = acc0 / l0 - lam * (acc1 / l1)
        o_ref[:, col:col + LANES] = (_rms(o, g_ref[...]) * (1.0 - lam0)).astype(o_ref.dtype)


def fox_prompt(q, k_t, v_t, c, c_t, tq):
    t, w = q.shape
    nb, _, s = k_t.shape
    nq = s // tq
    nh = c.shape[1]
    return pl.pallas_call(
        functools.partial(_fox_prompt_kernel, tq=tq, n_heads=nh),
        grid=(nb, nq),
        in_specs=[pl.BlockSpec((tq, w), lambda b, i: (b * nq + i, 0)),
                  pl.BlockSpec((1, w, s), lambda b, i: (b, 0, 0)),
                  pl.BlockSpec((1, w, s), lambda b, i: (b, 0, 0)),
                  pl.BlockSpec((tq, nh), lambda b, i: (b * nq + i, 0)),
                  pl.BlockSpec((1, nh, s), lambda b, i: (b, 0, 0))],
        out_specs=pl.BlockSpec((tq, w), lambda b, i: (b * nq + i, 0)),
        out_shape=jax.ShapeDtypeStruct((t, w), BF16),
        compiler_params=_params(2, 4 * s * w * 2 + 4 * tq * w * 2 + 16 * tq * tq * 4),
        name="fox_prompt",
    )(q, k_t, v_t, c, c_t)


def diff_prompt(q, k_t, v, lams, gain, lam0, tq):
    t, w = q.shape
    nb, _, s = k_t.shape
    nq = s // tq
    vw = v.shape[1]
    vec = pl.BlockSpec((1, HEAD_DIM), lambda b, i: (0, 0))
    return pl.pallas_call(
        functools.partial(_diff_prompt_kernel, tq=tq, n_heads=vw // LANES, lam0=lam0),
        grid=(nb, nq),
        in_specs=[pl.BlockSpec((tq, w), lambda b, i: (b * nq + i, 0)),
                  pl.BlockSpec((1, w, s), lambda b, i: (b, 0, 0)),
                  pl.BlockSpec((s, vw), lambda b, i: (b, 0)),
                  vec, vec, vec, vec,
                  pl.BlockSpec((1, LANES), lambda b, i: (0, 0))],
        out_specs=pl.BlockSpec((tq, vw), lambda b, i: (b * nq + i, 0)),
        out_shape=jax.ShapeDtypeStruct((t, vw), BF16),
        compiler_params=_params(2, 4 * s * w * 2 + 4 * tq * w * 2 + 16 * tq * tq * 4),
        name="diff_prompt",
    )(q, k_t, v, *[x.reshape(1, HEAD_DIM) for x in lams], gain.reshape(1, LANES))


def _sample_attn_kernel(pt_ref, *refs, fox, pages_per_step, n_steps, page, n_new, lam0):
    if fox:
        q_ref, kn_ref, vn_ref, c_ref, kc_ref, vc_ref, o_ref = refs[:7]
        scratch = refs[7:]
    else:
        q_ref, kn_ref, vn_ref, lq1, lk1, lq2, lk2, g_ref, kc_ref, vc_ref, o_ref = refs[:11]
        scratch = refs[11:]
    kbuf, vbuf, sem, qbd_ref, m_ref, l_ref, acc_ref, cq_ref = scratch

    b = pl.program_id(0)
    s = pl.program_id(1)
    nb = pl.num_programs(0)
    t = b * n_steps + s
    slot = t % 2
    w = q_ref.shape[1]
    n_rows = qbd_ref.shape[0]
    n_groups = n_rows // n_new
    n_heads_v = w // LANES
    n_past = n_steps * pages_per_step * page
    chunk = pages_per_step * page

    def copies(bb, ss, sl):
        out = []
        for p in range(pages_per_step):
            pg = pt_ref[bb, ss * pages_per_step + p]
            out.append(pltpu.make_async_copy(kc_ref.at[pg], kbuf.at[sl, :, pl.ds(p * page, page)],
                                             sem.at[sl, 0]))
            if fox:
                vdst = vbuf.at[sl, :, pl.ds(p * page, page)]
            else:
                vdst = vbuf.at[sl, pl.ds(p * page * n_heads_v, page * n_heads_v), :]
            out.append(pltpu.make_async_copy(vc_ref.at[pg], vdst, sem.at[sl, 1]))
        return out

    def start_all(cps):
        for n, cp in enumerate(cps):
            cp.start(priority=n % 2)

    @pl.when(t == 0)
    def _():
        start_all(copies(b, s, slot))

    @pl.when(t + 1 < nb * n_steps)
    def _():
        t1 = t + 1
        start_all(copies(t1 // n_steps, t1 % n_steps, t1 % 2))

    row = lax.broadcasted_iota(jnp.int32, (n_rows, 1), 0)

    def bias(col0, width):
        cb = c_ref[0, :, pl.ds(col0, width)]
        return jnp.broadcast_to(cb[:, None, :], (n_groups, n_new, width)).reshape(n_rows, width)

    @pl.when(s == 0)
    def _():
        qt = jnp.concatenate([q_ref[...]] * n_groups, axis=0)
        colg = lax.broadcasted_iota(jnp.int32, (1, w), 1) // HEAD_DIM
        qbd_ref[...] = jnp.where(row // n_new == colg, qt, 0.0).astype(BF16)
        m_ref[...] = jnp.full(m_ref.shape, -jnp.inf, F32)
        l_ref[...] = jnp.zeros(l_ref.shape, F32)
        acc_ref[...] = jnp.zeros(acc_ref.shape, F32)
        if fox:
            lane = lax.broadcasted_iota(jnp.int32, (1, page), 1)
            cq_ref[...] = jnp.sum(jnp.where(lane == row % n_new, bias(n_past, page), 0.0),
                                  axis=1, keepdims=True)
        else:
            cq_ref[...] = jnp.zeros(cq_ref.shape, F32)

    def update(sc, pv):
        cq = cq_ref[...]
        m_prev = m_ref[...]
        m_new = jnp.maximum(m_prev, cq + jnp.max(sc, axis=1, keepdims=True))
        p = jnp.exp(sc + (cq - m_new))
        alpha = jnp.exp(m_prev - m_new)
        l_ref[...] = alpha * l_ref[...] + jnp.sum(p, axis=1, keepdims=True)
        acc_ref[...] = alpha * acc_ref[...] + pv(p.astype(BF16))
        m_ref[...] = m_new

    def pv_heads(p, v_of_head):
        rows_per_head = 2 * n_new
        return jnp.concatenate(
            [_nn(p[h * rows_per_head:(h + 1) * rows_per_head], v_of_head(h)) for h in range(n_heads_v)],
            axis=0)

    for cp in copies(b, s, slot):
        cp.wait()

    sc = _nn(qbd_ref[...], kbuf[slot].astype(BF16))
    if fox:
        sc = sc - bias(pl.multiple_of(s * chunk, chunk), chunk)
        vt = vbuf[slot].astype(BF16)
        update(sc, lambda p: _nt(p, vt))
    else:
        update(sc, lambda p: pv_heads(
            p, lambda h: vbuf[slot, pl.ds(h, chunk, stride=n_heads_v), :].astype(BF16)))

    @pl.when(s == n_steps - 1)
    def _():
        pad = jnp.zeros((page - n_new, w), F32)
        kn = jnp.concatenate([kn_ref[...], pad], axis=0).astype(BF16)
        vn = jnp.concatenate([vn_ref[...], pad], axis=0).astype(BF16)
        sn = _nt(qbd_ref[...], kn)
        if fox:
            sn = sn - bias(n_past, page)
        lane = lax.broadcasted_iota(jnp.int32, (1, page), 1)
        sn = jnp.where(lane <= row % n_new, sn, NEG_INF)
        if fox:
            update(sn, lambda p: _nn(p, vn))
        else:
            update(sn, lambda p: pv_heads(p, lambda h: vn[:, h * LANES:(h + 1) * LANES]))

        on = acc_ref[...] / l_ref[...]
        if fox:
            col = lax.broadcasted_iota(jnp.int32, (1, 1, w), 2)
            grp = lax.broadcasted_iota(jnp.int32, (n_groups, 1, 1), 0)
            o3 = on.reshape(n_groups, n_new, w)
            o_ref[...] = jnp.sum(jnp.where(grp == col // HEAD_DIM, o3, 0.0), axis=0)
        else:
            lam = _lambda(lq1[...], lk1[...], lq2[...], lk2[...], lam0)
            for h in range(n_heads_v):
                r0 = 2 * h * n_new
                oh = on[r0:r0 + n_new] - lam * on[r0 + n_new:r0 + 2 * n_new]
                o_ref[:, h * LANES:(h + 1) * LANES] = _rms(oh, g_ref[...]) * (1.0 - lam0)


def sample_attention(page_table, q, k_new, v_new, k_cache_t, v_cache, pages_per_step, *,
                     c_all=None, lams=None, gain=None, lam0=0.0):
    fox = c_all is not None
    nb, n_pages = page_table.shape
    _, w, page = k_cache_t.shape
    n_new = q.shape[0] // nb
    n_steps = n_pages // pages_per_step
    chunk = pages_per_step * page
    n_rows = (w // HEAD_DIM) * n_new
    new_spec = pl.BlockSpec((n_new, w), lambda b, s, pt: (b, 0))
    any_spec = pl.BlockSpec(memory_space=pl.ANY)
    if fox:
        ins = [q, k_new, v_new, c_all, k_cache_t, v_cache]
        in_specs = [new_spec, new_spec, new_spec,
                    pl.BlockSpec((1,) + c_all.shape[1:], lambda b, s, pt: (b, 0, 0)),
                    any_spec, any_spec]
        v_scratch = pltpu.VMEM((2, w, chunk), F32)
        acc_cols = w
    else:
        vec = pl.BlockSpec((1, HEAD_DIM), lambda b, s, pt: (0, 0))
        ins = ([q, k_new, v_new] + [x.reshape(1, HEAD_DIM) for x in lams]
               + [gain.reshape(1, LANES), k_cache_t, v_cache])
        in_specs = [new_spec, new_spec, new_spec, vec, vec, vec, vec,
                    pl.BlockSpec((1, LANES), lambda b, s, pt: (0, 0)), any_spec, any_spec]
        v_scratch = pltpu.VMEM((2, chunk * (w // LANES), LANES), F32)
        acc_cols = LANES
    grid_spec = pltpu.PrefetchScalarGridSpec(
        num_scalar_prefetch=1, grid=(nb, n_steps), in_specs=in_specs,
        out_specs=new_spec,
        scratch_shapes=[pltpu.VMEM((2, w, chunk), F32), v_scratch,
                        pltpu.SemaphoreType.DMA((2, 2)),
                        pltpu.VMEM((n_rows, w), BF16),
                        pltpu.VMEM((n_rows, 1), F32), pltpu.VMEM((n_rows, 1), F32),
                        pltpu.VMEM((n_rows, acc_cols), F32), pltpu.VMEM((n_rows, 1), F32)])
    nbytes = 4 * chunk * w * 4 + 2 * chunk * w * 2 + 4 * n_rows * chunk * 4
    if fox:
        nbytes += 2 * c_all.shape[1] * c_all.shape[2] * 4
    return pl.pallas_call(
        functools.partial(_sample_attn_kernel, fox=fox, pages_per_step=pages_per_step, n_steps=n_steps,
                          page=page, n_new=n_new, lam0=lam0),
        grid_spec=grid_spec,
        out_shape=jax.ShapeDtypeStruct((nb * n_new, w), F32),
        compiler_params=_params(2, nbytes),
        name="fox_sample" if fox else "diff_sample",
    )(page_table, *ins)


def _merge_kernel(fo_ref, do_ref, wf_ref, wd_ref, gf_ref, gd_ref, o_ref, wfb, wdb):
    @pl.when(pl.program_id(1) == 0)
    def _():
        wfb[...] = wf_ref[0].astype(BF16)
        wdb[...] = wd_ref[0].astype(BF16)

    pf = _nn(fo_ref[...].astype(BF16), wfb[...])
    pd = _nn(do_ref[...].astype(BF16), wdb[...])
    o_ref[...] = (jax.nn.sigmoid(gf_ref[...]) * pf + jax.nn.sigmoid(gd_ref[...]) * pd).astype(o_ref.dtype)


def branch_merge(fo, do, w_branch, gates, tm, tn):
    t, bw = fo.shape
    d = w_branch.shape[2]
    nn = d // tn
    nbytes = 4 * tm * bw * fo.dtype.itemsize + 4 * bw * tn * 4 + 2 * bw * tn * 2 + 6 * tm * tn * 4
    return pl.pallas_call(
        _merge_kernel, grid=(nn, t // tm),
        in_specs=[pl.BlockSpec((tm, bw), lambda j, i: (i, 0)),
                  pl.BlockSpec((tm, bw), lambda j, i: (i, 0)),
                  pl.BlockSpec((1, bw, tn), lambda j, i: (0, 0, j)),
                  pl.BlockSpec((1, bw, tn), lambda j, i: (1, 0, j)),
                  pl.BlockSpec((tm, tn), lambda j, i: (i, j)),
                  pl.BlockSpec((tm, tn), lambda j, i: (i, nn + j))],
        out_specs=pl.BlockSpec((tm, tn), lambda j, i: (i, j)),
        out_shape=jax.ShapeDtypeStruct((t, d), BF16),
        scratch_shapes=[pltpu.VMEM((bw, tn), BF16), pltpu.VMEM((bw, tn), BF16)],
        compiler_params=_params(2, nbytes),
        name="branch_merge",
    )(fo, do, w_branch, w_branch, gates, gates)


def _gelu_tanh(x):
    return 0.5 * x * (1.0 + jnp.tanh(math.sqrt(2.0 / math.pi) * (x + 0.044715 * (x * x * x))))


def _causal_conv(u, cw, prev1, prev2, period):
    n = u.shape[0]
    pos = lax.broadcasted_iota(jnp.int32, (n, 1), 0) % period
    u1 = jnp.where(pos == 0, prev1, pltpu.roll(u, 1, 0))
    u2 = jnp.where(pos == 0, prev2, jnp.where(pos == 1, prev1, pltpu.roll(u, 2, 0)))
    return cw[0:1] * u2 + cw[1:2] * u1 + cw[2:3] * u


def _ffn_prompt_kernel(h_ref, wg_ref, wv_ref, cwg_ref, cwv_ref, wd_ref, f_ref, cs_ref,
                       carry_ref, *, tiles_per_seq, sub):
    i = pl.program_id(0)
    f = pl.program_id(1)
    tm = h_ref.shape[0]
    tf = wg_ref.shape[1]
    h = h_ref[...]

    @pl.when(i % tiles_per_seq == 0)
    def _():
        carry_ref[f] = jnp.zeros(carry_ref.shape[1:], F32)

    @pl.when(f == 0)
    def _():
        f_ref[...] = jnp.zeros(f_ref.shape, F32)

    acts = []
    for c in range(tf // sub):
        cs = slice(c * sub, (c + 1) * sub)
        ug = _nn(h, wg_ref[:, cs])
        uv = _nn(h, wv_ref[:, cs])
        pg = carry_ref[f, 0, :, cs]
        pv = carry_ref[f, 1, :, cs]
        yg = _causal_conv(ug, cwg_ref[:, cs], pg[SUBLANES - 1:SUBLANES], pg[SUBLANES - 2:SUBLANES - 1], tm)
        yv = _causal_conv(uv, cwv_ref[:, cs], pv[SUBLANES - 1:SUBLANES], pv[SUBLANES - 2:SUBLANES - 1], tm)
        carry_ref[f, 0, :, cs] = ug[tm - SUBLANES:tm]
        carry_ref[f, 1, :, cs] = uv[tm - SUBLANES:tm]
        cs_ref[0, 0, :, cs] = ug[tm - (CONV_WIDTH - 1):tm]
        cs_ref[0, 1, :, cs] = uv[tm - (CONV_WIDTH - 1):tm]
        acts.append((_gelu_tanh(yg) * yv).astype(BF16))

    f_ref[...] += _nn(jnp.concatenate(acts, axis=1), wd_ref[...])


def cast_bf16(w):
    n, m = w.shape
    rows = _tile(n, max(SUBLANES, (4 * 1024 * 1024) // (4 * m)))

    def kern(w_ref, o_ref):
        o_ref[...] = w_ref[...].astype(BF16)

    return pl.pallas_call(
        kern, grid=(n // rows,),
        in_specs=[pl.BlockSpec((rows, m), lambda i: (i, 0))],
        out_specs=pl.BlockSpec((rows, m), lambda i: (i, 0)),
        out_shape=jax.ShapeDtypeStruct((n, m), BF16),
        compiler_params=_params(1, 2 * rows * m * 6),
        name="cast_bf16",
    )(w)


def ffn_prompt(h, w_up, conv_w, w_down, seq, tm, tf, sub):
    t, d = h.shape
    dff = w_down.shape[0]
    nf = dff // tf
    tiles_per_seq = seq // tm
    nbytes = (2 * tm * d * 2 + 4 * d * tf * 2 + 2 * tf * d * 2 + 2 * tm * d * 4
              + 10 * tm * tf * 4 + nf * 2 * SUBLANES * tf * 4)
    return pl.pallas_call(
        functools.partial(_ffn_prompt_kernel, tiles_per_seq=tiles_per_seq, sub=sub),
        grid=(t // tm, nf),
        in_specs=[pl.BlockSpec((tm, d), lambda i, f: (i, 0)),
                  pl.BlockSpec((d, tf), lambda i, f: (0, f)),
                  pl.BlockSpec((d, tf), lambda i, f: (0, nf + f)),
                  pl.BlockSpec((CONV_WIDTH, tf), lambda i, f: (0, f)),
                  pl.BlockSpec((CONV_WIDTH, tf), lambda i, f: (0, nf + f)),
                  pl.BlockSpec((tf, d), lambda i, f: (f, 0))],
        out_specs=[pl.BlockSpec((tm, d), lambda i, f: (i, 0)),
                   pl.BlockSpec((1, 2, CONV_WIDTH - 1, tf), lambda i, f: (i, 0, 0, f))],
        out_shape=[jax.ShapeDtypeStruct((t, d), F32),
                   jax.ShapeDtypeStruct((t // tm, 2, CONV_WIDTH - 1, dff), F32)],
        scratch_shapes=[pltpu.VMEM((nf, 2, SUBLANES, tf), F32)],
        compiler_params=_params(2, nbytes),
        name="ffn_prompt",
    )(h, w_up, w_up, conv_w, conv_w, w_down)


def _ffn_sample_kernel(h_ref, wg_ref, wv_ref, cwg_ref, cwv_ref, wd_ref, sg_ref, sv_ref, f_ref, cs_ref,
                       *, n_new):
    f = pl.program_id(0)
    t = h_ref.shape[0]
    nb = t // n_new
    tf = wg_ref.shape[1]
    h = h_ref[...]
    ug = _nn(h, wg_ref[...])
    uv = _nn(h, wv_ref[...])

    def rows_of(state, r):
        return jnp.broadcast_to(state[:, r:r + 1, :], (nb, n_new, tf)).reshape(t, tf)

    sg = sg_ref[...]
    sv = sv_ref[...]
    yg = _causal_conv(ug, cwg_ref[...], rows_of(sg, 1), rows_of(sg, 0), n_new)
    yv = _causal_conv(uv, cwv_ref[...], rows_of(sv, 1), rows_of(sv, 0), n_new)
    keep = CONV_WIDTH - 1
    cs_ref[0] = ug.reshape(nb, n_new, tf)[:, n_new - keep:, :]
    cs_ref[1] = uv.reshape(nb, n_new, tf)[:, n_new - keep:, :]

    act = (_gelu_tanh(yg) * yv).astype(BF16)

    @pl.when(f == 0)
    def _():
        f_ref[...] = jnp.zeros(f_ref.shape, F32)

    f_ref[...] += _nn(act, wd_ref[...])


def ffn_sample(h, w_up, conv_w, w_down, state, n_new, tf):
    t, d = h.shape
    dff = w_down.shape[0]
    nf = dff // tf
    nb = t // n_new
    keep = CONV_WIDTH - 1
    nbytes = (2 * t * d * 2 + 4 * d * tf * 2 + 2 * tf * d * 2 + 2 * t * d * 4
              + 12 * t * tf * 4 + 8 * nb * SUBLANES * tf * 4)
    return pl.pallas_call(
        functools.partial(_ffn_sample_kernel, n_new=n_new),
        grid=(nf,),
        in_specs=[pl.BlockSpec((t, d), lambda f: (0, 0)),
                  pl.BlockSpec((d, tf), lambda f: (0, f)),
                  pl.BlockSpec((d, tf), lambda f: (0, nf + f)),
                  pl.BlockSpec((CONV_WIDTH, tf), lambda f: (0, f)),
                  pl.BlockSpec((CONV_WIDTH, tf), lambda f: (0, nf + f)),
                  pl.BlockSpec((tf, d), lambda f: (f, 0)),
                  pl.BlockSpec((nb, keep, tf), lambda f: (0, 0, f)),
                  pl.BlockSpec((nb, keep, tf), lambda f: (0, 0, nf + f))],
        out_specs=[pl.BlockSpec((t, d), lambda f: (0, 0)),
                   pl.BlockSpec((2, nb, keep, tf), lambda f: (0, 0, 0, f))],
        out_shape=[jax.ShapeDtypeStruct((t, d), F32),
                   jax.ShapeDtypeStruct((2, nb, keep, dff), F32)],
        compiler_params=_params(1, nbytes),
        name="ffn_sample",
    )(h, w_up, w_up, conv_w, conv_w, w_down, state, state)


def _lambda_init(layer_idx):
    return 0.8 - 0.6 * math.exp(-0.3 * layer_idx)


def _rope_angles(pos):
    half = HEAD_DIM // 2
    inv_freq = ROPE_THETA ** (-jnp.arange(half, dtype=F32) / half)
    ang = pos.astype(F32)[:, None] * inv_freq[None, :]
    return jnp.cos(ang), jnp.sin(ang)


def _rope_tables(pos):
    cos, sin = _rope_angles(pos)
    return (jnp.concatenate([cos, cos, cos, cos], axis=1),
            jnp.concatenate([-sin, sin, -sin, sin], axis=1))


def _out_proj_kernel(a_ref, w_ref, x_ref, g1_ref, g2_ref, o_ref, h_ref):
    mix = _nn(a_ref[...], w_ref[...])
    y = x_ref[...] + _rms(mix, g1_ref[...])
    o_ref[...] = y
    h_ref[...] = _rms(y, g2_ref[...]).astype(h_ref.dtype)


def out_proj_norm(merged, w_out_b, x, g1, g2, tm):
    t, d = x.shape
    row = pl.BlockSpec((tm, d), lambda i: (i, 0))
    vec = pl.BlockSpec((1, d), lambda i: (0, 0))
    return pl.pallas_call(
        _out_proj_kernel, grid=(t // tm,),
        in_specs=[row, pl.BlockSpec((d, d), lambda i: (0, 0)), row, vec, vec],
        out_specs=[row, row],
        out_shape=[jax.ShapeDtypeStruct((t, d), F32), jax.ShapeDtypeStruct((t, d), BF16)],
        compiler_params=_params(1, 2 * d * d * 2 + 2 * tm * d * (2 + 4 + 4 + 2) + 2 * tm * d * 4),
        name="out_proj_norm",
    )(merged, w_out_b, x, g1.reshape(1, d), g2.reshape(1, d))


def _mixer_tail(x, fo, do, gates, w_branch_l, w_out_b, g_post, g_ffn_pre, tm, tn):
    merged = branch_merge(fo, do, w_branch_l, gates, tm, tn)
    return out_proj_norm(merged, w_out_b, x, g_post, g_ffn_pre, _tile(x.shape[0], 512))


def kernel(x_prompt, x_sample, cache_fox_k, cache_fox_v, cache_fox_logf, cache_diff_k, cache_diff_v,
           state_ffn_conv, page_table, norm_mix_pre, norm_mix_post, w_in, b_forget, lam_q1, lam_k1,
           lam_q2, lam_k2, diff_subln, w_branch, w_out, norm_ffn_pre, norm_ffn_post, w_up, conv_ffn,
           w_down):
    nbp, seq, d = x_prompt.shape
    nbs, n_new, _ = x_sample.shape
    depth, n_pool, page, fox_heads, _ = cache_fox_k.shape
    diff_heads = cache_diff_v.shape[3]
    fw = fox_heads * HEAD_DIM
    dqk = 2 * diff_heads * HEAD_DIM
    dvw = diff_heads * 2 * HEAD_DIM
    dff = w_down.shape[1]
    n_pages = page_table.shape[1]
    n_past = n_pages * page
    tp, ts = nbp * seq, nbs * n_new

    tm = _tile(seq, ROW_TILE)
    tq = _tile(seq, ATTN_TILE)
    tf = _tile(dff, FF_TILE)
    tn = _tile(fw, COL_TILE)
    tn_d = _tile(d, COL_TILE)
    pages_per_step = _tile(n_pages, PAGES_PER_STEP)
    assert dqk == fw and dvw == fw and fw % tn == 0 and (2 * d) % tn == 0

    tps = seq // tm
    cos_p, sin_p = _rope_tables(jnp.arange(seq))
    cos_pt, sin_pt = [x.T for x in _rope_angles(jnp.arange(seq))]
    cos_s, sin_s = _rope_tables(n_past + jnp.tile(jnp.arange(n_new), nbs))

    def rope_extras(cos, sin, rows, period):
        spec = pl.BlockSpec((rows, LANES), lambda j, i: (i % period, 0))
        return [(cos, spec), (sin, spec)]

    xp = x_prompt.reshape(tp, d)
    xs = x_sample.reshape(ts, d)
    outs_p = [[] for _ in range(6)]
    outs_s = [[] for _ in range(6)]
    nblk = fw // tn
    for l in range(depth):
        lam0 = _lambda_init(l)
        lams = (lam_q1[l], lam_k1[l], lam_q2[l], lam_k2[l])
        w_t = w_in[l].T
        w_ff_t = w_t[3 * fw:3 * fw + fox_heads]
        w_rest_t = w_t[3 * fw + fox_heads:]
        proj = functools.partial(matmul, w_transposed=True, n_blocks=nblk, tn=tn)

        hp = rmsnorm_bf16(xp, norm_mix_pre[l], tm)
        pj = functools.partial(proj, hp, tm=tm)
        fm_pair = [_feature_out(nbp, fw, seq, tm, tn, F32), _feature_out(nbp, fw, seq, tm, tn, BF16)]
        fq = pj(w_t, col_block=0, outs=[_token_out(tp, fw, tm, tn, BF16)],
                epilogue=_epi_plain(Q_SCALE * LOG2E))[0]
        fk, fkb = pj(w_t, col_block=nblk, outs=fm_pair, epilogue=_epi_plain(1.0, True), feature_major=True)
        fv, fvb = pj(w_t, col_block=2 * nblk, outs=fm_pair, epilogue=_epi_plain(1.0, True),
                     feature_major=True)
        logf = forget_gate(hp, w_ff_t, b_forget[l], seq, tm)
        dq = pj(w_rest_t, col_block=0, outs=[_token_out(tp, dqk, tm, tn, BF16)],
                epilogue=_epi_rope(Q_SCALE * LOG2E), extras=rope_extras(cos_p, sin_p, tm, tps))[0]
        fm_spec = pl.BlockSpec((HEAD_DIM // 2, tm), lambda j, i: (0, i % tps))
        dk, dkb = pj(w_rest_t, col_block=nblk, outs=fm_pair, epilogue=_epi_rope_fm,
                     extras=[(cos_pt, fm_spec), (sin_pt, fm_spec)], feature_major=True)
        dv, dvb = pj(w_rest_t, col_block=2 * nblk,
                     outs=[(jax.ShapeDtypeStruct((tp, dvw // LANES, LANES), F32),
                            pl.BlockSpec((tm, tn // LANES, LANES), lambda j, i: (i, j, 0))),
                           _token_out(tp, dvw, tm, tn, BF16)],
                     epilogue=_epi_heads)
        gates = matmul(hp, w_rest_t, w_transposed=True, col_block=3 * nblk, n_blocks=2 * d // tn, tn=tn,
                       tm=tm, outs=[_token_out(tp, 2 * d, tm, tn, F32)], epilogue=_epi_plain(1.0))[0]
        c_t, c = cumsum_prompt(logf, tq)
        fo = fox_prompt(fq, fkb, fvb, c, c_t, tq)
        do = diff_prompt(dq, dkb, dvb, lams, diff_subln[l], lam0, tq)
        w_out_b = cast_bf16(w_out[l])
        x2, h2 = _mixer_tail(xp, fo, do, gates, w_branch[l], w_out_b, norm_mix_post[l], norm_ffn_pre[l],
                             tm, tn_d)
        w_up_b = cast_bf16(w_up[l])
        w_down_b = cast_bf16(w_down[l])
        f, conv_p = ffn_prompt(h2, w_up_b, conv_ffn[l], w_down_b, seq, tm, tf, _tile(tf, FF_SUB))
        xp = residual_norm(x2, f, norm_ffn_post[l], _tile(tp, 512))
        conv_p = conv_p[tps - 1::tps]
        conv_p = jnp.transpose(conv_p, (0, 2, 1, 3)).reshape(nbp, CONV_WIDTH - 1, 2 * dff)

        def heads_last(x_t, n_heads):
            return jnp.transpose(x_t.reshape(nbp, n_heads, HEAD_DIM, seq), (0, 3, 1, 2))

        for lst, val in zip(outs_p, (heads_last(fk, fox_heads), heads_last(fv, fox_heads),
                                     jnp.transpose(logf, (0, 2, 1)),
                                     heads_last(dk, 2 * diff_heads),
                                     dv.reshape(nbp, seq, diff_heads, 2 * HEAD_DIM), conv_p)):
            lst.append(val)

        hs = rmsnorm_bf16(xs, norm_mix_pre[l], ts)
        pj = functools.partial(proj, hs, tm=ts)
        tok = lambda n: [_token_out(ts, n, ts, tn, F32)]
        fq = pj(w_t, col_block=0, outs=tok(fw), epilogue=_epi_plain(Q_SCALE))[0]
        fk = pj(w_t, col_block=nblk, outs=tok(fw), epilogue=_epi_plain(1.0))[0]
        fv = pj(w_t, col_block=2 * nblk, outs=tok(fw), epilogue=_epi_plain(1.0))[0]
        logf = forget_gate(hs, w_ff_t, b_forget[l], ts, ts)[0]
        dq = pj(w_rest_t, col_block=0, outs=tok(dqk), epilogue=_epi_rope(Q_SCALE),
                extras=rope_extras(cos_s, sin_s, ts, 1))[0]
        dk = pj(w_rest_t, col_block=nblk, outs=tok(dqk), epilogue=_epi_rope(1.0),
                extras=rope_extras(cos_s, sin_s, ts, 1))[0]
        dv = pj(w_rest_t, col_block=2 * nblk, outs=tok(dvw), epilogue=_epi_plain(1.0))[0]
        gates = matmul(hs, w_rest_t, w_transposed=True, col_block=3 * nblk, n_blocks=2 * d // tn, tn=tn,
                       tm=ts, outs=[_token_out(ts, 2 * d, ts, tn, F32)], epilogue=_epi_plain(1.0))[0]
        feature_major = lambda cache: jnp.transpose(cache, (0, 2, 3, 1)).reshape(n_pool, -1, page)
        c_all = cumsum_sample(page_table, logf, jnp.transpose(cache_fox_logf[l], (0, 2, 1)), n_new)
        fo = sample_attention(page_table, fq, fk, fv, feature_major(cache_fox_k[l]),
                              feature_major(cache_fox_v[l]), pages_per_step, c_all=c_all)
        do = sample_attention(page_table, dq, dk, dv, feature_major(cache_diff_k[l]),
                              cache_diff_v[l].reshape(n_pool, page * diff_heads, 2 * HEAD_DIM),
                              pages_per_step, lams=lams, gain=diff_subln[l], lam0=lam0)
        x2, h2 = _mixer_tail(xs, fo, do, gates, w_branch[l], w_out_b, norm_mix_post[l], norm_ffn_pre[l],
                             ts, tn_d)
        f, conv_s = ffn_sample(h2, w_up_b, conv_ffn[l], w_down_b, state_ffn_conv[l], n_new, tf)
        xs = residual_norm(x2, f, norm_ffn_post[l], ts)
        conv_s = jnp.transpose(conv_s, (1, 2, 0, 3)).reshape(nbs, CONV_WIDTH - 1, 2 * dff)
        for lst, val in zip(outs_s, (fk.reshape(nbs, n_new, fox_heads, HEAD_DIM),
                                     fv.reshape(nbs, n_new, fox_heads, HEAD_DIM),
                                     logf.T.reshape(nbs, n_new, fox_heads),
                                     dk.reshape(nbs, n_new, 2 * diff_heads, HEAD_DIM),
                                     dv.reshape(nbs, n_new, diff_heads, 2 * HEAD_DIM), conv_s)):
            lst.append(val)

    return (xp.reshape(nbp, seq, d), xs.reshape(nbs, n_new, d),
            *[jnp.stack(v) for v in outs_p], *[jnp.stack(v) for v in outs_s])
```

```python
import functools
import math

import jax
import jax.numpy as jnp
from jax import lax
from jax.experimental import pallas as pl
from jax.experimental.pallas import tpu as pltpu

HEAD_DIM = 64
CONV_WIDTH = 3
ROPE_THETA = 10000.0
RMS_EPS = 1e-6
NEG_INF = -1e30
Q_SCALE = HEAD_DIM ** -0.5
LOG2E = math.log2(math.e)

LANES = 128
SUBLANES = 8
V7X_VMEM_BYTES = 64 * 1024 * 1024
VMEM_CAP_BYTES = V7X_VMEM_BYTES - 8 * 1024 * 1024

F32 = jnp.float32
BF16 = jnp.bfloat16

ROW_TILE = 1024
COL_TILE = 1024
ATTN_TILE = 512
ATTN_GROUP = 2
FF_TILE = 1024
FF_SUB = 256
MM_SUB = 256
PAGES_PER_STEP = 8


def _vmem(nbytes):
    return int(min(VMEM_CAP_BYTES, max(32 * 1024 * 1024, 2 * nbytes)))


def _params(n_axes, vmem_bytes):
    return pltpu.CompilerParams(dimension_semantics=("arbitrary",) * n_axes,
                                vmem_limit_bytes=_vmem(vmem_bytes))


def _nt(a, b):
    return lax.dot_general(a, b, (((1,), (1,)), ((), ())), preferred_element_type=F32)


def _nn(a, b):
    return jnp.dot(a, b, preferred_element_type=F32)


def _rms(x, g):
    return x * lax.rsqrt(jnp.mean(x * x, axis=-1, keepdims=True) + RMS_EPS) * g


def _tile(n, pref):
    t = min(n, pref)
    while n % t:
        t //= 2
    return t


def _rmsnorm_kernel(x_ref, g_ref, o_ref):
    o_ref[...] = _rms(x_ref[...], g_ref[...]).astype(o_ref.dtype)


def rmsnorm_bf16(x, g, tm):
    t, d = x.shape
    return pl.pallas_call(
        _rmsnorm_kernel,
        grid=(t // tm,),
        in_specs=[pl.BlockSpec((tm, d), lambda i: (i, 0)),
                  pl.BlockSpec((1, d), lambda i: (0, 0))],
        out_specs=pl.BlockSpec((tm, d), lambda i: (i, 0)),
        out_shape=jax.ShapeDtypeStruct((t, d), BF16),
        compiler_params=_params(1, 2 * tm * d * 6),
        name="rmsnorm_bf16",
    )(x, g.reshape(1, d))


def _resnorm_kernel(x_ref, z_ref, g1_ref, o_ref):
    o_ref[...] = x_ref[...] + _rms(z_ref[...], g1_ref[...])


def residual_norm(x, z, g1, tm):
    t, d = x.shape
    row = pl.BlockSpec((tm, d), lambda i: (i, 0))
    vec = pl.BlockSpec((1, d), lambda i: (0, 0))
    return pl.pallas_call(
        _resnorm_kernel, grid=(t // tm,), in_specs=[row, row, vec], out_specs=row,
        out_shape=jax.ShapeDtypeStruct((t, d), F32),
        compiler_params=_params(1, 2 * tm * d * 12), name="residual_norm",
    )(x, z, g1.reshape(1, d))


def _mm_kernel(*refs, n_extra, n_out, epilogue, w_transposed, feature_major):
    a_ref, w_ref = refs[0], refs[1]
    extra = refs[2:2 + n_extra]
    outs = refs[2 + n_extra:2 + n_extra + n_out]
    wb_ref = refs[-1]

    @pl.when(pl.program_id(1) == 0)
    def _():
        wb_ref[...] = w_ref[...].astype(BF16)

    a = a_ref[...].astype(BF16)
    tn = wb_ref.shape[0] if w_transposed else wb_ref.shape[1]
    sub = _tile(tn, MM_SUB)
    for c0 in range(0, tn, sub):
        if feature_major:
            acc = _nt(wb_ref[c0:c0 + sub, :], a)
        elif w_transposed:
            acc = _nt(a, wb_ref[c0:c0 + sub, :])
        else:
            acc = _nn(a, wb_ref[:, c0:c0 + sub])
        epilogue(acc, extra, outs, c0)


def _epi_plain(scale, feature_major=False):
    def epi(acc, extra, outs, c0):
        val = acc if scale == 1.0 else acc * scale
        for o in outs:
            if feature_major:
                o[0, c0:c0 + acc.shape[0], :] = val.astype(o.dtype)
            else:
                o[:, c0:c0 + acc.shape[1]] = val.astype(o.dtype)
    return epi


def _epi_heads(acc, extra, outs, c0):
    native, dense = outs
    for h in range(acc.shape[1] // LANES):
        native[:, c0 // LANES + h, :] = acc[:, h * LANES:(h + 1) * LANES]
    dense[:, c0:c0 + acc.shape[1]] = acc.astype(dense.dtype)


def _epi_rope(scale):
    def epi(acc, extra, outs, c0):
        cos = extra[0][...]
        sin = extra[1][...]
        first_half = (lax.broadcasted_iota(jnp.int32, (1, LANES), 1) % HEAD_DIM) < HEAD_DIM // 2
        for c in range(acc.shape[1] // LANES):
            x = acc[:, c * LANES:(c + 1) * LANES]
            swapped = jnp.where(first_half, pltpu.roll(x, LANES - HEAD_DIM // 2, 1),
                                pltpu.roll(x, HEAD_DIM // 2, 1))
            val = x * cos + swapped * sin
            if scale != 1.0:
                val = val * scale
            for o in outs:
                o[:, c0 + c * LANES:c0 + (c + 1) * LANES] = val.astype(o.dtype)
    return epi


def _epi_rope_fm(acc, extra, outs, c0):
    cos = extra[0][...]
    sin = extra[1][...]
    half = HEAD_DIM // 2
    for m in range(acc.shape[0] // HEAD_DIM):
        x1 = acc[m * HEAD_DIM:m * HEAD_DIM + half]
        x2 = acc[m * HEAD_DIM + half:(m + 1) * HEAD_DIM]
        y1 = x1 * cos - x2 * sin
        y2 = x2 * cos + x1 * sin
        r0 = c0 + m * HEAD_DIM
        for o in outs:
            o[0, r0:r0 + half, :] = y1.astype(o.dtype)
            o[0, r0 + half:r0 + HEAD_DIM, :] = y2.astype(o.dtype)


def matmul(a, w, *, w_transposed, col_block, n_blocks, tn, tm, outs, epilogue, extras=(),
           feature_major=False, row_offset=0):
    t, k = a.shape
    in_specs = [pl.BlockSpec((tm, k), lambda j, i: (i, 0))]
    if w_transposed:
        in_specs.append(pl.BlockSpec((pl.Element(tn), pl.Element(k)),
                                     lambda j, i: (pl.multiple_of(row_offset + (col_block + j) * tn,
                                                                  SUBLANES), 0)))
        w_scratch = pltpu.VMEM((tn, k), BF16)
    else:
        in_specs.append(pl.BlockSpec((k, tn), lambda j, i: (0, col_block + j)))
        w_scratch = pltpu.VMEM((k, tn), BF16)
    in_specs += [spec for _, spec in extras]
    nbytes = 2 * tm * k * a.dtype.itemsize + 2 * k * tn * 4 + k * tn * 2 + 2 * tm * tn * 4
    nbytes += sum(2 * tm * tn * s.dtype.itemsize for s, _ in outs)
    return pl.pallas_call(
        functools.partial(_mm_kernel, n_extra=len(extras), n_out=len(outs), epilogue=epilogue,
                          w_transposed=w_transposed, feature_major=feature_major),
        grid=(n_blocks, t // tm),
        in_specs=in_specs,
        out_specs=[spec for _, spec in outs],
        out_shape=[s for s, _ in outs],
        scratch_shapes=[w_scratch],
        compiler_params=_params(2, nbytes),
        name="proj_matmul",
    )(a, w, *[x for x, _ in extras])


def _token_out(t, n, tm, tn, dtype):
    return jax.ShapeDtypeStruct((t, n), dtype), pl.BlockSpec((tm, tn), lambda j, i: (i, j))


def _feature_out(nb, n, seq, tm, tn, dtype):
    tps = seq // tm
    return (jax.ShapeDtypeStruct((nb, n, seq), dtype),
            pl.BlockSpec((1, tn, tm), lambda j, i: (i // tps, j, i % tps)))


def _log_sigmoid(x):
    return jnp.minimum(x, 0.0) - jnp.log1p(jnp.exp(-jnp.abs(x)))


def _logf_kernel(a_ref, w_ref, b_ref, o_ref):
    acc = _nt(w_ref[...].astype(BF16), a_ref[...])
    o_ref[0] = _log_sigmoid(acc + b_ref[...])


def forget_gate(a, w_ff_t, b_forget, seq, tm):
    t, k = a.shape
    nh = w_ff_t.shape[0]
    tps = seq // tm
    return pl.pallas_call(
        _logf_kernel, grid=(t // tm,),
        in_specs=[pl.BlockSpec((tm, k), lambda i: (i, 0)),
                  pl.BlockSpec((nh, k), lambda i: (0, 0)),
                  pl.BlockSpec((nh, 1), lambda i: (0, 0))],
        out_specs=pl.BlockSpec((1, nh, tm), lambda i: (i // tps, 0, i % tps)),
        out_shape=jax.ShapeDtypeStruct((t // seq, nh, seq), F32),
        compiler_params=_params(1, 2 * tm * k * 2 + k * LANES * 8),
        name="forget_gate",
    )(a, w_ff_t, b_forget.reshape(nh, 1))


def _scan_lanes(x):
    n = x.shape[1]
    lane = lax.broadcasted_iota(jnp.int32, (1, n), 1)
    shift = 1
    while shift < n:
        x = x + jnp.where(lane >= shift, pltpu.roll(x, shift, 1), 0.0)
        shift *= 2
    return x


def _split3(c):
    hi = c.astype(BF16)
    r1 = c - hi.astype(F32)
    mid = r1.astype(BF16)
    lo = (r1 - mid.astype(F32)).astype(BF16)
    return hi, mid, lo


def _cumsum_prompt_kernel(lf_ref, ct_ref, c_ref, *, blk):
    c = _scan_lanes(lf_ref[0]) * LOG2E
    ct_ref[0] = c
    eye = (lax.broadcasted_iota(jnp.int32, (blk, blk), 0)
           == lax.broadcasted_iota(jnp.int32, (blk, blk), 1)).astype(BF16)
    for j in range(c.shape[1] // blk):
        hi, mid, lo = _split3(c[:, j * blk:(j + 1) * blk])
        c_ref[j * blk:(j + 1) * blk, :] = _nt(eye, hi) + _nt(eye, mid) + _nt(eye, lo)


def cumsum_prompt(logf_t, blk):
    b, nh, s = logf_t.shape
    return pl.pallas_call(
        functools.partial(_cumsum_prompt_kernel, blk=blk), grid=(b,),
        in_specs=[pl.BlockSpec((1, nh, s), lambda i: (i, 0, 0))],
        out_specs=[pl.BlockSpec((1, nh, s), lambda i: (i, 0, 0)),
                   pl.BlockSpec((s, nh), lambda i: (i, 0))],
        out_shape=[jax.ShapeDtypeStruct((b, nh, s), F32), jax.ShapeDtypeStruct((b * s, nh), F32)],
        compiler_params=_params(1, 8 * s * LANES * 4),
        name="cumsum_prompt",
    )(logf_t)


def _cumsum_sample_kernel(pt_ref, lfn_ref, cache_ref, c_ref, buf, sem, *, n_pages, page, n_new):
    b = pl.program_id(0)
    slot = b % 2

    def page_copies(bb, sl):
        return [pltpu.make_async_copy(cache_ref.at[pt_ref[bb, p]], buf.at[sl, :, pl.ds(p * page, page)],
                                      sem.at[sl]) for p in range(n_pages)]

    @pl.when(b == 0)
    def _():
        for cp in page_copies(b, slot):
            cp.start()

    @pl.when(b + 1 < pl.num_programs(0))
    def _():
        for cp in page_copies(b + 1, 1 - slot):
            cp.start()

    n_tok = lfn_ref.shape[1]
    tok = lax.broadcasted_iota(jnp.int32, (n_tok, 1), 0)
    j = lax.broadcasted_iota(jnp.int32, (1, page), 1)
    sel = ((tok == b * n_new + j) & (j < n_new)).astype(BF16)
    hi, mid, lo = _split3(lfn_ref[...])
    buf[slot, :, n_pages * page:] = _nn(hi, sel) + _nn(mid, sel) + _nn(lo, sel)
    for cp in page_copies(b, slot):
        cp.wait()
    c_ref[0] = _scan_lanes(buf[slot])


def cumsum_sample(page_table, logf_new_t, cache_logf_t, n_new):
    nb, n_pages = page_table.shape
    _, nh, page = cache_logf_t.shape
    n_cols = n_pages * page + page
    grid_spec = pltpu.PrefetchScalarGridSpec(
        num_scalar_prefetch=1, grid=(nb,),
        in_specs=[pl.BlockSpec(logf_new_t.shape, lambda i, pt: (0, 0)),
                  pl.BlockSpec(memory_space=pl.ANY)],
        out_specs=pl.BlockSpec((1, nh, n_cols), lambda i, pt: (i, 0, 0)),
        scratch_shapes=[pltpu.VMEM((2, nh, n_cols), F32), pltpu.SemaphoreType.DMA((2,))])
    return pl.pallas_call(
        functools.partial(_cumsum_sample_kernel, n_pages=n_pages, page=page, n_new=n_new),
        grid_spec=grid_spec,
        out_shape=jax.ShapeDtypeStruct((nb, nh, n_cols), F32),
        compiler_params=_params(1, 24 * nh * n_cols * 4),
        name="cumsum_sample",
    )(page_table, logf_new_t, cache_logf_t)


def _flash_pairs(q2s, kv_fns, i, tq, cqs, ck_fns, sum_rows):
    lane = lax.broadcasted_iota(jnp.int32, (1, LANES), 1)
    qms = []
    for q2 in q2s:
        qms += [jnp.where(lane < HEAD_DIM, q2, jnp.zeros_like(q2)),
                jnp.where(lane >= HEAD_DIM, q2, jnp.zeros_like(q2))]

    rows = lax.broadcasted_iota(jnp.int32, (tq, tq), 0)
    cols = lax.broadcasted_iota(jnp.int32, (tq, tq), 1)

    def block(j, carry, diagonal):
        r0 = pl.multiple_of(j * tq, tq)
        out = []
        for n, kv_fn in enumerate(kv_fns):
            kt, pvs = kv_fn(r0)
            for a in range(2):
                c = 2 * n + a
                m, l, acc = carry[c]
                s = _nn(qms[c], kt)
                if ck_fns is not None:
                    s = s - ck_fns[c](r0)
                if diagonal:
                    s = jnp.where(cols <= rows, s, NEG_INF)
                row_max = jnp.max(s, axis=1, keepdims=True)
                m_new = jnp.maximum(m, row_max if cqs is None else cqs[c] + row_max)
                p = jnp.exp2(s + ((-m_new) if cqs is None else (cqs[c] - m_new)))
                alpha = jnp.exp2(m - m_new)
                if sum_rows:
                    l = alpha * l + jnp.sum(p, axis=1, keepdims=True)
                acc = alpha * acc + pvs[a](p.astype(BF16))
                out.append((m_new, l, acc))
        return tuple(out)

    init = (jnp.full((tq, 1), -jnp.inf, F32), jnp.zeros((tq, 1), F32), jnp.zeros((tq, LANES), F32))
    carry = lax.fori_loop(0, i, lambda j, c: block(j, c, False), (init,) * (2 * len(q2s)))
    carry = block(i, carry, True)
    return [(acc, l) for _, l, acc in carry]


def _fox_prompt_kernel(q_ref, k_ref, v_ref, c_ref, ct_ref, o_ref, *, tq, n_heads, group):
    i = pl.program_id(1)
    lane = lax.broadcasted_iota(jnp.int32, (1, LANES), 1)
    head = lax.broadcasted_iota(jnp.int32, (1, n_heads), 1)
    feat = lax.broadcasted_iota(jnp.int32, (LANES, 1), 0)
    c_blk = c_ref[...]

    def make_kv_fn(col):
        def kv_fn(r0):
            vt = v_ref[0, col:col + LANES, pl.ds(r0, tq)]
            vts = [jnp.where(feat < HEAD_DIM, vt, jnp.ones_like(vt)),
                   jnp.where(feat >= HEAD_DIM, vt, jnp.ones_like(vt))]
            return (k_ref[0, col:col + LANES, pl.ds(r0, tq)],
                    [lambda p, v=v: _nt(p, v) for v in vts])
        return kv_fn

    for g in range(0, n_heads // 2, group):
        slabs = range(g, min(g + group, n_heads // 2))
        hs = [2 * hp + a for hp in slabs for a in range(2)]
        cqs = [jnp.sum(jnp.where(head == h, c_blk, 0.0), axis=1, keepdims=True) for h in hs]
        ck_fns = [lambda r0, h=h: ct_ref[0, h:h + 1, pl.ds(r0, tq)] for h in hs]
        res = _flash_pairs([q_ref[:, hp * LANES:(hp + 1) * LANES] for hp in slabs],
                           [make_kv_fn(hp * LANES) for hp in slabs], i, tq, cqs, ck_fns, False)
        for n, hp in enumerate(slabs):
            acc0, acc1 = res[2 * n][0], res[2 * n + 1][0]
            o0 = acc0 / pltpu.roll(acc0, HEAD_DIM, 1)
            o1 = acc1 / pltpu.roll(acc1, HEAD_DIM, 1)
            o_ref[:, hp * LANES:(hp + 1) * LANES] = jnp.where(lane < HEAD_DIM, o0, o1).astype(o_ref.dtype)


def _lambda(lq1, lk1, lq2, lk2, lam0):
    return (jnp.exp(jnp.sum(lq1 * lk1, axis=1, keepdims=True))
            - jnp.exp(jnp.sum(lq2 * lk2, axis=1, keepdims=True)) + lam0)


def _diff_prompt_kernel(q_ref, k_ref, v_ref, lq1, lk1, lq2, lk2, g_ref, o_ref, *, tq, n_heads, lam0,
                        group):
    i = pl.program_id(1)
    lam = _lambda(lq1[...], lk1[...], lq2[...], lk2[...], lam0)

    def make_kv_fn(col):
        def kv_fn(r0):
            vb = v_ref[pl.ds(r0, tq), col:col + LANES]
            return k_ref[0, col:col + LANES, pl.ds(r0, tq)], [lambda p: _nn(p, vb)] * 2
        return kv_fn

    for g in range(0, n_heads, group):
        slabs = range(g, min(g + group, n_heads))
        res = _flash_pairs([q_ref[:, h * LANES:(h + 1) * LANES] for h in slabs],
                           [make_kv_fn(h * LANES) for h in slabs], i, tq, None, None, True)
        for n, h in enumerate(slabs):
            (acc0, l0), (acc1, l1) = res[2 * n], res[2 * n + 1]
            o = acc0 / l0 - lam * (acc1 / l1)
            o_ref[:, h * LANES:(h + 1) * LANES] = (_rms(o, g_ref[...]) * (1.0 - lam0)).astype(o_ref.dtype)


def fox_prompt(q, k_t, v_t, c, c_t, tq):
    t, w = q.shape
    nb, _, s = k_t.shape
    nq = s // tq
    nh = c.shape[1]
    return pl.pallas_call(
        functools.partial(_fox_prompt_kernel, tq=tq, n_heads=nh, group=ATTN_GROUP),
        grid=(nb, nq),
        in_specs=[pl.BlockSpec((tq, w), lambda b, i: (b * nq + i, 0)),
                  pl.BlockSpec((1, w, s), lambda b, i: (b, 0, 0)),
                  pl.BlockSpec((1, w, s), lambda b, i: (b, 0, 0)),
                  pl.BlockSpec((tq, nh), lambda b, i: (b * nq + i, 0)),
                  pl.BlockSpec((1, nh, s), lambda b, i: (b, 0, 0))],
        out_specs=pl.BlockSpec((tq, w), lambda b, i: (b * nq + i, 0)),
        out_shape=jax.ShapeDtypeStruct((t, w), BF16),
        compiler_params=_params(2, 4 * s * w * 2 + 4 * tq * w * 2 + 16 * tq * tq * 4),
        name="fox_prompt",
    )(q, k_t, v_t, c, c_t)


def diff_prompt(q, k_t, v, lams, gain, lam0, tq):
    t, w = q.shape
    nb, _, s = k_t.shape
    nq = s // tq
    vw = v.shape[1]
    vec = pl.BlockSpec((1, HEAD_DIM), lambda b, i: (0, 0))
    return pl.pallas_call(
        functools.partial(_diff_prompt_kernel, tq=tq, n_heads=vw // LANES, lam0=lam0, group=ATTN_GROUP),
        grid=(nb, nq),
        in_specs=[pl.BlockSpec((tq, w), lambda b, i: (b * nq + i, 0)),
                  pl.BlockSpec((1, w, s), lambda b, i: (b, 0, 0)),
                  pl.BlockSpec((s, vw), lambda b, i: (b, 0)),
                  vec, vec, vec, vec,
                  pl.BlockSpec((1, LANES), lambda b, i: (0, 0))],
        out_specs=pl.BlockSpec((tq, vw), lambda b, i: (b * nq + i, 0)),
        out_shape=jax.ShapeDtypeStruct((t, vw), BF16),
        compiler_params=_params(2, 4 * s * w * 2 + 4 * tq * w * 2 + 16 * tq * tq * 4),
        name="diff_prompt",
    )(q, k_t, v, *[x.reshape(1, HEAD_DIM) for x in lams], gain.reshape(1, LANES))


def _sample_attn_kernel(pt_ref, *refs, fox, pages_per_step, n_steps, page, n_new, lam0):
    n_in = 6 if fox else 10
    b = pl.program_id(0)
    s = pl.program_id(1)
    nb = pl.num_programs(0)
    t = b * n_steps + s
    copies, step = _paged_attention(fox, pt_ref, refs[:n_in], refs[n_in], refs[n_in + 1:],
                                    pages_per_step=pages_per_step, n_steps=n_steps, page=page,
                                    n_new=n_new, lam0=lam0)

    @pl.when(t == 0)
    def _():
        _start_all(copies(b, s, t % 2))

    @pl.when(t + 1 < nb * n_steps)
    def _():
        t1 = t + 1
        _start_all(copies(t1 // n_steps, t1 % n_steps, t1 % 2))

    step(b, s, t % 2)


def _start_all(cps):
    for n, cp in enumerate(cps):
        cp.start(priority=n % 2)


def _paged_attention(fox, pt_ref, ins, o_ref, scratch, *, pages_per_step, n_steps, page, n_new, lam0):
    if fox:
        q_ref, kn_ref, vn_ref, c_ref, kc_ref, vc_ref = ins
    else:
        q_ref, kn_ref, vn_ref, lq1, lk1, lq2, lk2, g_ref, kc_ref, vc_ref = ins
    kbuf, vbuf, sem, qbd_ref, m_ref, l_ref, acc_ref, cq_ref = scratch
    w = q_ref.shape[1]
    n_rows = qbd_ref.shape[0]
    n_groups = n_rows // n_new
    n_heads_v = w // LANES
    n_past = n_steps * pages_per_step * page
    chunk = pages_per_step * page

    def copies(bb, ss, sl):
        out = []
        for p in range(pages_per_step):
            pg = pt_ref[bb, ss * pages_per_step + p]
            out.append(pltpu.make_async_copy(kc_ref.at[pg], kbuf.at[sl, :, pl.ds(p * page, page)],
                                             sem.at[sl, 0]))
            if fox:
                vdst = vbuf.at[sl, :, pl.ds(p * page, page)]
            else:
                vdst = vbuf.at[sl, pl.ds(p * page * n_heads_v, page * n_heads_v), :]
            out.append(pltpu.make_async_copy(vc_ref.at[pg], vdst, sem.at[sl, 1]))
        return out

    row = lax.broadcasted_iota(jnp.int32, (n_rows, 1), 0)

    def bias(col0, width):
        cb = c_ref[0, :, pl.ds(col0, width)]
        return jnp.broadcast_to(cb[:, None, :], (n_groups, n_new, width)).reshape(n_rows, width)

    def init():
        qt = jnp.concatenate([q_ref[...]] * n_groups, axis=0)
        colg = lax.broadcasted_iota(jnp.int32, (1, w), 1) // HEAD_DIM
        qbd_ref[...] = jnp.where(row // n_new == colg, qt, 0.0).astype(BF16)
        m_ref[...] = jnp.full(m_ref.shape, -jnp.inf, F32)
        l_ref[...] = jnp.zeros(l_ref.shape, F32)
        acc_ref[...] = jnp.zeros(acc_ref.shape, F32)
        if fox:
            lane = lax.broadcasted_iota(jnp.int32, (1, page), 1)
            cq_ref[...] = jnp.sum(jnp.where(lane == row % n_new, bias(n_past, page), 0.0),
                                  axis=1, keepdims=True)
        else:
            cq_ref[...] = jnp.zeros(cq_ref.shape, F32)

    def update(sc, pv):
        cq = cq_ref[...]
        m_prev = m_ref[...]
        m_new = jnp.maximum(m_prev, cq + jnp.max(sc, axis=1, keepdims=True))
        p = jnp.exp(sc + (cq - m_new))
        alpha = jnp.exp(m_prev - m_new)
        l_ref[...] = alpha * l_ref[...] + jnp.sum(p, axis=1, keepdims=True)
        acc_ref[...] = alpha * acc_ref[...] + pv(p.astype(BF16))
        m_ref[...] = m_new

    def pv_heads(p, v_of_head):
        rows_per_head = 2 * n_new
        return jnp.concatenate(
            [_nn(p[h * rows_per_head:(h + 1) * rows_per_head], v_of_head(h)) for h in range(n_heads_v)],
            axis=0)

    def finish():
        pad = jnp.zeros((page - n_new, w), F32)
        kn = jnp.concatenate([kn_ref[...], pad], axis=0).astype(BF16)
        vn = jnp.concatenate([vn_ref[...], pad], axis=0).astype(BF16)
        sn = _nt(qbd_ref[...], kn)
        if fox:
            sn = sn - bias(n_past, page)
        lane = lax.broadcasted_iota(jnp.int32, (1, page), 1)
        sn = jnp.where(lane <= row % n_new, sn, NEG_INF)
        if fox:
            update(sn, lambda p: _nn(p, vn))
        else:
            update(sn, lambda p: pv_heads(p, lambda h: vn[:, h * LANES:(h + 1) * LANES]))

        on = acc_ref[...] / l_ref[...]
        if fox:
            col = lax.broadcasted_iota(jnp.int32, (1, 1, w), 2)
            grp = lax.broadcasted_iota(jnp.int32, (n_groups, 1, 1), 0)
            o3 = on.reshape(n_groups, n_new, w)
            o_ref[...] = jnp.sum(jnp.where(grp == col // HEAD_DIM, o3, 0.0), axis=0)
        else:
            lam = _lambda(lq1[...], lk1[...], lq2[...], lk2[...], lam0)
            for h in range(n_heads_v):
                r0 = 2 * h * n_new
                oh = on[r0:r0 + n_new] - lam * on[r0 + n_new:r0 + 2 * n_new]
                o_ref[:, h * LANES:(h + 1) * LANES] = _rms(oh, g_ref[...]) * (1.0 - lam0)

    def step(b, s, slot):
        pl.when(s == 0)(init)
        for cp in copies(b, s, slot):
            cp.wait()
        sc = _nn(qbd_ref[...], kbuf[slot].astype(BF16))
        if fox:
            sc = sc - bias(pl.multiple_of(s * chunk, chunk), chunk)
            vt = vbuf[slot].astype(BF16)
            update(sc, lambda p: _nt(p, vt))
        else:
            update(sc, lambda p: pv_heads(
                p, lambda h: vbuf[slot, pl.ds(h, chunk, stride=n_heads_v), :].astype(BF16)))
        pl.when(s == n_steps - 1)(finish)

    return copies, step


def sample_attention(page_table, q, k_new, v_new, k_cache_t, v_cache, pages_per_step, *,
                     c_all=None, lams=None, gain=None, lam0=0.0):
    fox = c_all is not None
    nb, n_pages = page_table.shape
    _, w, page = k_cache_t.shape
    n_new = q.shape[0] // nb
    n_steps = n_pages // pages_per_step
    chunk = pages_per_step * page
    n_rows = (w // HEAD_DIM) * n_new
    new_spec = pl.BlockSpec((n_new, w), lambda b, s, pt: (b, 0))
    any_spec = pl.BlockSpec(memory_space=pl.ANY)
    if fox:
        ins = [q, k_new, v_new, c_all, k_cache_t, v_cache]
        in_specs = [new_spec, new_spec, new_spec,
                    pl.BlockSpec((1,) + c_all.shape[1:], lambda b, s, pt: (b, 0, 0)),
                    any_spec, any_spec]
        v_scratch = pltpu.VMEM((2, w, chunk), F32)
        acc_cols = w
    else:
        vec = pl.BlockSpec((1, HEAD_DIM), lambda b, s, pt: (0, 0))
        ins = ([q, k_new, v_new] + [x.reshape(1, HEAD_DIM) for x in lams]
               + [gain.reshape(1, LANES), k_cache_t, v_cache])
        in_specs = [new_spec, new_spec, new_spec, vec, vec, vec, vec,
                    pl.BlockSpec((1, LANES), lambda b, s, pt: (0, 0)), any_spec, any_spec]
        v_scratch = pltpu.VMEM((2, chunk * (w // LANES), LANES), F32)
        acc_cols = LANES
    grid_spec = pltpu.PrefetchScalarGridSpec(
        num_scalar_prefetch=1, grid=(nb, n_steps), in_specs=in_specs,
        out_specs=new_spec,
        scratch_shapes=[pltpu.VMEM((2, w, chunk), F32), v_scratch,
                        pltpu.SemaphoreType.DMA((2, 2)),
                        pltpu.VMEM((n_rows, w), BF16),
                        pltpu.VMEM((n_rows, 1), F32), pltpu.VMEM((n_rows, 1), F32),
                        pltpu.VMEM((n_rows, acc_cols), F32), pltpu.VMEM((n_rows, 1), F32)])
    nbytes = 4 * chunk * w * 4 + 2 * chunk * w * 2 + 4 * n_rows * chunk * 4
    if fox:
        nbytes += 2 * c_all.shape[1] * c_all.shape[2] * 4
    return pl.pallas_call(
        functools.partial(_sample_attn_kernel, fox=fox, pages_per_step=pages_per_step, n_steps=n_steps,
                          page=page, n_new=n_new, lam0=lam0),
        grid_spec=grid_spec,
        out_shape=jax.ShapeDtypeStruct((nb * n_new, w), F32),
        compiler_params=_params(2, nbytes),
        name="fox_sample" if fox else "diff_sample",
    )(page_table, *ins)


def _merge_kernel(fo_ref, do_ref, wf_ref, wd_ref, gf_ref, gd_ref, o_ref, wfb, wdb):
    @pl.when(pl.program_id(1) == 0)
    def _():
        wfb[...] = wf_ref[0].astype(BF16)
        wdb[...] = wd_ref[0].astype(BF16)

    pf = _nn(fo_ref[...].astype(BF16), wfb[...])
    pd = _nn(do_ref[...].astype(BF16), wdb[...])
    o_ref[...] = (jax.nn.sigmoid(gf_ref[...]) * pf + jax.nn.sigmoid(gd_ref[...]) * pd).astype(o_ref.dtype)


def branch_merge(fo, do, w_branch, gates, tm, tn):
    t, bw = fo.shape
    d = w_branch.shape[2]
    nn = d // tn
    nbytes = 4 * tm * bw * fo.dtype.itemsize + 4 * bw * tn * 4 + 2 * bw * tn * 2 + 6 * tm * tn * 4
    return pl.pallas_call(
        _merge_kernel, grid=(nn, t // tm),
        in_specs=[pl.BlockSpec((tm, bw), lambda j, i: (i, 0)),
                  pl.BlockSpec((tm, bw), lambda j, i: (i, 0)),
                  pl.BlockSpec((1, bw, tn), lambda j, i: (0, 0, j)),
                  pl.BlockSpec((1, bw, tn), lambda j, i: (1, 0, j)),
                  pl.BlockSpec((tm, tn), lambda j, i: (i, j)),
                  pl.BlockSpec((tm, tn), lambda j, i: (i, nn + j))],
        out_specs=pl.BlockSpec((tm, tn), lambda j, i: (i, j)),
        out_shape=jax.ShapeDtypeStruct((t, d), BF16),
        scratch_shapes=[pltpu.VMEM((bw, tn), BF16), pltpu.VMEM((bw, tn), BF16)],
        compiler_params=_params(2, nbytes),
        name="branch_merge",
    )(fo, do, w_branch, w_branch, gates, gates)


def _gelu_tanh(x):
    return 0.5 * x * (1.0 + jnp.tanh(math.sqrt(2.0 / math.pi) * (x + 0.044715 * (x * x * x))))


def _causal_conv(u, cw, prev1, prev2, period):
    n = u.shape[0]
    pos = lax.broadcasted_iota(jnp.int32, (n, 1), 0) % period
    u1 = jnp.where(pos == 0, prev1, pltpu.roll(u, 1, 0))
    u2 = jnp.where(pos == 0, prev2, jnp.where(pos == 1, prev1, pltpu.roll(u, 2, 0)))
    return cw[0:1] * u2 + cw[1:2] * u1 + cw[2:3] * u


def _ffn_prompt_kernel(h_ref, wg_ref, wv_ref, cwg_ref, cwv_ref, wd_ref, f_ref, cs_ref,
                       carry_ref, *, tiles_per_seq, sub):
    i = pl.program_id(0)
    f = pl.program_id(1)
    tm = h_ref.shape[0]
    tf = wg_ref.shape[1]
    h = h_ref[...]

    @pl.when(i % tiles_per_seq == 0)
    def _():
        carry_ref[f] = jnp.zeros(carry_ref.shape[1:], F32)

    @pl.when(f == 0)
    def _():
        f_ref[...] = jnp.zeros(f_ref.shape, F32)

    acts = []
    for c in range(tf // sub):
        cs = slice(c * sub, (c + 1) * sub)
        ug = _nn(h, wg_ref[:, cs])
        uv = _nn(h, wv_ref[:, cs])
        pg = carry_ref[f, 0, :, cs]
        pv = carry_ref[f, 1, :, cs]
        yg = _causal_conv(ug, cwg_ref[:, cs], pg[SUBLANES - 1:SUBLANES], pg[SUBLANES - 2:SUBLANES - 1], tm)
        yv = _causal_conv(uv, cwv_ref[:, cs], pv[SUBLANES - 1:SUBLANES], pv[SUBLANES - 2:SUBLANES - 1], tm)
        carry_ref[f, 0, :, cs] = ug[tm - SUBLANES:tm]
        carry_ref[f, 1, :, cs] = uv[tm - SUBLANES:tm]
        cs_ref[0, 0, :, cs] = ug[tm - (CONV_WIDTH - 1):tm]
        cs_ref[0, 1, :, cs] = uv[tm - (CONV_WIDTH - 1):tm]
        acts.append((_gelu_tanh(yg) * yv).astype(BF16))

    f_ref[...] += _nn(jnp.concatenate(acts, axis=1), wd_ref[...])


def cast_bf16(w):
    n, m = w.shape
    rows = _tile(n, max(SUBLANES, (4 * 1024 * 1024) // (4 * m)))

    def kern(w_ref, o_ref):
        o_ref[...] = w_ref[...].astype(BF16)

    return pl.pallas_call(
        kern, grid=(n // rows,),
        in_specs=[pl.BlockSpec((rows, m), lambda i: (i, 0))],
        out_specs=pl.BlockSpec((rows, m), lambda i: (i, 0)),
        out_shape=jax.ShapeDtypeStruct((n, m), BF16),
        compiler_params=_params(1, 2 * rows * m * 6),
        name="cast_bf16",
    )(w)


def ffn_prompt(h, w_up, conv_w, w_down, seq, tm, tf, sub):
    t, d = h.shape
    dff = w_down.shape[0]
    nf = dff // tf
    tiles_per_seq = seq // tm
    nbytes = (2 * tm * d * 2 + 4 * d * tf * 2 + 2 * tf * d * 2 + 2 * tm * d * 4
              + 10 * tm * tf * 4 + nf * 2 * SUBLANES * tf * 4)
    return pl.pallas_call(
        functools.partial(_ffn_prompt_kernel, tiles_per_seq=tiles_per_seq, sub=sub),
        grid=(t // tm, nf),
        in_specs=[pl.BlockSpec((tm, d), lambda i, f: (i, 0)),
                  pl.BlockSpec((d, tf), lambda i, f: (0, f)),
                  pl.BlockSpec((d, tf), lambda i, f: (0, nf + f)),
                  pl.BlockSpec((CONV_WIDTH, tf), lambda i, f: (0, f)),
                  pl.BlockSpec((CONV_WIDTH, tf), lambda i, f: (0, nf + f)),
                  pl.BlockSpec((tf, d), lambda i, f: (f, 0))],
        out_specs=[pl.BlockSpec((tm, d), lambda i, f: (i, 0)),
                   pl.BlockSpec((1, 2, CONV_WIDTH - 1, tf), lambda i, f: (i, 0, 0, f))],
        out_shape=[jax.ShapeDtypeStruct((t, d), F32),
                   jax.ShapeDtypeStruct((t // tm, 2, CONV_WIDTH - 1, dff), F32)],
        scratch_shapes=[pltpu.VMEM((nf, 2, SUBLANES, tf), F32)],
        compiler_params=_params(2, nbytes),
        name="ffn_prompt",
    )(h, w_up, w_up, conv_w, conv_w, w_down)


def _ffn_sample_kernel(h_ref, wg_ref, wv_ref, cwg_ref, cwv_ref, wd_ref, sg_ref, sv_ref, f_ref, cs_ref,
                       *, n_new):
    f = pl.program_id(0)
    t = h_ref.shape[0]
    nb = t // n_new
    tf = wg_ref.shape[1]
    h = h_ref[...]
    ug = _nn(h, wg_ref[...])
    uv = _nn(h, wv_ref[...])

    def rows_of(state, r):
        return jnp.broadcast_to(state[:, r:r + 1, :], (nb, n_new, tf)).reshape(t, tf)

    sg = sg_ref[...]
    sv = sv_ref[...]
    yg = _causal_conv(ug, cwg_ref[...], rows_of(sg, 1), rows_of(sg, 0), n_new)
    yv = _causal_conv(uv, cwv_ref[...], rows_of(sv, 1), rows_of(sv, 0), n_new)
    keep = CONV_WIDTH - 1
    cs_ref[0] = ug.reshape(nb, n_new, tf)[:, n_new - keep:, :]
    cs_ref[1] = uv.reshape(nb, n_new, tf)[:, n_new - keep:, :]

    act = (_gelu_tanh(yg) * yv).astype(BF16)

    @pl.when(f == 0)
    def _():
        f_ref[...] = jnp.zeros(f_ref.shape, F32)

    f_ref[...] += _nn(act, wd_ref[...])


def ffn_sample(h, w_up, conv_w, w_down, state, n_new, tf):
    t, d = h.shape
    dff = w_down.shape[0]
    nf = dff // tf
    nb = t // n_new
    keep = CONV_WIDTH - 1
    nbytes = (2 * t * d * 2 + 4 * d * tf * 2 + 2 * tf * d * 2 + 2 * t * d * 4
              + 12 * t * tf * 4 + 8 * nb * SUBLANES * tf * 4)
    return pl.pallas_call(
        functools.partial(_ffn_sample_kernel, n_new=n_new),
        grid=(nf,),
        in_specs=[pl.BlockSpec((t, d), lambda f: (0, 0)),
                  pl.BlockSpec((d, tf), lambda f: (0, f)),
                  pl.BlockSpec((d, tf), lambda f: (0, nf + f)),
                  pl.BlockSpec((CONV_WIDTH, tf), lambda f: (0, f)),
                  pl.BlockSpec((CONV_WIDTH, tf), lambda f: (0, nf + f)),
                  pl.BlockSpec((tf, d), lambda f: (f, 0)),
                  pl.BlockSpec((nb, keep, tf), lambda f: (0, 0, f)),
                  pl.BlockSpec((nb, keep, tf), lambda f: (0, 0, nf + f))],
        out_specs=[pl.BlockSpec((t, d), lambda f: (0, 0)),
                   pl.BlockSpec((2, nb, keep, tf), lambda f: (0, 0, 0, f))],
        out_shape=[jax.ShapeDtypeStruct((t, d), F32),
                   jax.ShapeDtypeStruct((2, nb, keep, dff), F32)],
        compiler_params=_params(1, nbytes),
        name="ffn_sample",
    )(h, w_up, w_up, conv_w, conv_w, w_down, state, state)


def _lambda_init(layer_idx):
    return 0.8 - 0.6 * math.exp(-0.3 * layer_idx)


def _rope_angles(pos):
    half = HEAD_DIM // 2
    inv_freq = ROPE_THETA ** (-jnp.arange(half, dtype=F32) / half)
    ang = pos.astype(F32)[:, None] * inv_freq[None, :]
    return jnp.cos(ang), jnp.sin(ang)


def _rope_tables(pos):
    cos, sin = _rope_angles(pos)
    return (jnp.concatenate([cos, cos, cos, cos], axis=1),
            jnp.concatenate([-sin, sin, -sin, sin], axis=1))


def _out_proj_kernel(a_ref, w_ref, x_ref, g1_ref, g2_ref, o_ref, h_ref):
    mix = _nn(a_ref[...], w_ref[...])
    y = x_ref[...] + _rms(mix, g1_ref[...])
    o_ref[...] = y
    h_ref[...] = _rms(y, g2_ref[...]).astype(h_ref.dtype)


def out_proj_norm(merged, w_out_b, x, g1, g2, tm):
    t, d = x.shape
    row = pl.BlockSpec((tm, d), lambda i: (i, 0))
    vec = pl.BlockSpec((1, d), lambda i: (0, 0))
    return pl.pallas_call(
        _out_proj_kernel, grid=(t // tm,),
        in_specs=[row, pl.BlockSpec((d, d), lambda i: (0, 0)), row, vec, vec],
        out_specs=[row, row],
        out_shape=[jax.ShapeDtypeStruct((t, d), F32), jax.ShapeDtypeStruct((t, d), BF16)],
        compiler_params=_params(1, 2 * d * d * 2 + 2 * tm * d * (2 + 4 + 4 + 2) + 2 * tm * d * 4),
        name="out_proj_norm",
    )(merged, w_out_b, x, g1.reshape(1, d), g2.reshape(1, d))


def _mixer_tail(x, fo, do, gates, w_branch_l, w_out_b, g_post, g_ffn_pre, tm, tn):
    merged = branch_merge(fo, do, w_branch_l, gates, tm, tn)
    return out_proj_norm(merged, w_out_b, x, g_post, g_ffn_pre, _tile(x.shape[0], 512))


def kernel(x_prompt, x_sample, cache_fox_k, cache_fox_v, cache_fox_logf, cache_diff_k, cache_diff_v,
           state_ffn_conv, page_table, norm_mix_pre, norm_mix_post, w_in, b_forget, lam_q1, lam_k1,
           lam_q2, lam_k2, diff_subln, w_branch, w_out, norm_ffn_pre, norm_ffn_post, w_up, conv_ffn,
           w_down):
    nbp, seq, d = x_prompt.shape
    nbs, n_new, _ = x_sample.shape
    depth, n_pool, page, fox_heads, _ = cache_fox_k.shape
    diff_heads = cache_diff_v.shape[3]
    fw = fox_heads * HEAD_DIM
    dqk = 2 * diff_heads * HEAD_DIM
    dvw = diff_heads * 2 * HEAD_DIM
    dff = w_down.shape[1]
    n_pages = page_table.shape[1]
    n_past = n_pages * page
    tp, ts = nbp * seq, nbs * n_new

    tm = _tile(seq, ROW_TILE)
    tq = _tile(seq, ATTN_TILE)
    tf = _tile(dff, FF_TILE)
    tn = _tile(fw, COL_TILE)
    tn_d = _tile(d, COL_TILE)
    pages_per_step = _tile(n_pages, PAGES_PER_STEP)
    assert dqk == fw and dvw == fw and fw % tn == 0 and (2 * d) % tn == 0

    tps = seq // tm
    cos_p, sin_p = _rope_tables(jnp.arange(seq))
    cos_pt, sin_pt = [x.T for x in _rope_angles(jnp.arange(seq))]
    cos_s, sin_s = _rope_tables(n_past + jnp.tile(jnp.arange(n_new), nbs))

    def rope_extras(cos, sin, rows, period):
        spec = pl.BlockSpec((rows, LANES), lambda j, i: (i % period, 0))
        return [(cos, spec), (sin, spec)]

    xp = x_prompt.reshape(tp, d)
    xs = x_sample.reshape(ts, d)
    outs_p = [[] for _ in range(6)]
    outs_s = [[] for _ in range(6)]
    nblk = fw // tn
    for l in range(depth):
        lam0 = _lambda_init(l)
        lams = (lam_q1[l], lam_k1[l], lam_q2[l], lam_k2[l])
        w_t = w_in[l].T
        w_ff_t = w_t[3 * fw:3 * fw + fox_heads]
        rest_row0 = 3 * fw + fox_heads
        proj = functools.partial(matmul, w_transposed=True, n_blocks=nblk, tn=tn)

        hp = rmsnorm_bf16(xp, norm_mix_pre[l], tm)
        pj = functools.partial(proj, hp, tm=tm)
        fm_pair = [_feature_out(nbp, fw, seq, tm, tn, F32), _feature_out(nbp, fw, seq, tm, tn, BF16)]
        fq = pj(w_t, col_block=0, outs=[_token_out(tp, fw, tm, tn, BF16)],
                epilogue=_epi_plain(Q_SCALE * LOG2E))[0]
        fk, fkb = pj(w_t, col_block=nblk, outs=fm_pair, epilogue=_epi_plain(1.0, True), feature_major=True)
        fv, fvb = pj(w_t, col_block=2 * nblk, outs=fm_pair, epilogue=_epi_plain(1.0, True),
                     feature_major=True)
        logf = forget_gate(hp, w_ff_t, b_forget[l], seq, tm)
        dq = pj(w_t, row_offset=rest_row0,col_block=0, outs=[_token_out(tp, dqk, tm, tn, BF16)],
                epilogue=_epi_rope(Q_SCALE * LOG2E), extras=rope_extras(cos_p, sin_p, tm, tps))[0]
        fm_spec = pl.BlockSpec((HEAD_DIM // 2, tm), lambda j, i: (0, i % tps))
        dk, dkb = pj(w_t, row_offset=rest_row0,col_block=nblk, outs=fm_pair, epilogue=_epi_rope_fm,
                     extras=[(cos_pt, fm_spec), (sin_pt, fm_spec)], feature_major=True)
        dv, dvb = pj(w_t, row_offset=rest_row0,col_block=2 * nblk,
                     outs=[(jax.ShapeDtypeStruct((tp, dvw // LANES, LANES), F32),
                            pl.BlockSpec((tm, tn // LANES, LANES), lambda j, i: (i, j, 0))),
                           _token_out(tp, dvw, tm, tn, BF16)],
                     epilogue=_epi_heads)
        gates = matmul(hp, w_t, row_offset=rest_row0, w_transposed=True, col_block=3 * nblk, n_blocks=2 * d // tn, tn=tn,
                       tm=tm, outs=[_token_out(tp, 2 * d, tm, tn, F32)], epilogue=_epi_plain(1.0))[0]
        c_t, c = cumsum_prompt(logf, tq)
        fo = fox_prompt(fq, fkb, fvb, c, c_t, tq)
        do = diff_prompt(dq, dkb, dvb, lams, diff_subln[l], lam0, tq)
        w_out_b = cast_bf16(w_out[l])
        x2, h2 = _mixer_tail(xp, fo, do, gates, w_branch[l], w_out_b, norm_mix_post[l], norm_ffn_pre[l],
                             tm, tn_d)
        w_up_b = cast_bf16(w_up[l])
        w_down_b = cast_bf16(w_down[l])
        f, conv_p = ffn_prompt(h2, w_up_b, conv_ffn[l], w_down_b, seq, tm, tf, _tile(tf, FF_SUB))
        xp = residual_norm(x2, f, norm_ffn_post[l], _tile(tp, 512))
        conv_p = conv_p[tps - 1::tps]
        conv_p = jnp.transpose(conv_p, (0, 2, 1, 3)).reshape(nbp, CONV_WIDTH - 1, 2 * dff)

        def heads_last(x_t, n_heads):
            return jnp.transpose(x_t.reshape(nbp, n_heads, HEAD_DIM, seq), (0, 3, 1, 2))

        for lst, val in zip(outs_p, (heads_last(fk, fox_heads), heads_last(fv, fox_heads),
                                     jnp.transpose(logf, (0, 2, 1)),
                                     heads_last(dk, 2 * diff_heads),
                                     dv.reshape(nbp, seq, diff_heads, 2 * HEAD_DIM), conv_p)):
            lst.append(val)

        hs = rmsnorm_bf16(xs, norm_mix_pre[l], ts)
        pj = functools.partial(proj, hs, tm=ts)
        tok = lambda n: [_token_out(ts, n, ts, tn, F32)]
        fq = pj(w_t, col_block=0, outs=tok(fw), epilogue=_epi_plain(Q_SCALE))[0]
        fk = pj(w_t, col_block=nblk, outs=tok(fw), epilogue=_epi_plain(1.0))[0]
        fv = pj(w_t, col_block=2 * nblk, outs=tok(fw), epilogue=_epi_plain(1.0))[0]
        logf = forget_gate(hs, w_ff_t, b_forget[l], ts, ts)[0]
        dq = pj(w_t, row_offset=rest_row0,col_block=0, outs=tok(dqk), epilogue=_epi_rope(Q_SCALE),
                extras=rope_extras(cos_s, sin_s, ts, 1))[0]
        dk = pj(w_t, row_offset=rest_row0,col_block=nblk, outs=tok(dqk), epilogue=_epi_rope(1.0),
                extras=rope_extras(cos_s, sin_s, ts, 1))[0]
        dv = pj(w_t, row_offset=rest_row0,col_block=2 * nblk, outs=tok(dvw), epilogue=_epi_plain(1.0))[0]
        gates = matmul(hs, w_t, row_offset=rest_row0, w_transposed=True, col_block=3 * nblk, n_blocks=2 * d // tn, tn=tn,
                       tm=ts, outs=[_token_out(ts, 2 * d, ts, tn, F32)], epilogue=_epi_plain(1.0))[0]
        feature_major = lambda cache: jnp.transpose(cache, (0, 2, 3, 1)).reshape(n_pool, -1, page)
        c_all = cumsum_sample(page_table, logf, jnp.transpose(cache_fox_logf[l], (0, 2, 1)), n_new)
        fo = sample_attention(page_table, fq, fk, fv, feature_major(cache_fox_k[l]),
                              feature_major(cache_fox_v[l]), pages_per_step, c_all=c_all)
        do = sample_attention(page_table, dq, dk, dv, feature_major(cache_diff_k[l]),
                              cache_diff_v[l].reshape(n_pool, page * diff_heads, 2 * HEAD_DIM),
                              pages_per_step, lams=lams, gain=diff_subln[l], lam0=lam0)
        x2, h2 = _mixer_tail(xs, fo, do, gates, w_branch[l], w_out_b, norm_mix_post[l], norm_ffn_pre[l],
                             ts, tn_d)
        f, conv_s = ffn_sample(h2, w_up_b, conv_ffn[l], w_down_b, state_ffn_conv[l], n_new, tf)
        xs = residual_norm(x2, f, norm_ffn_post[l], ts)
        conv_s = jnp.transpose(conv_s, (1, 2, 0, 3)).reshape(nbs, CONV_WIDTH - 1, 2 * dff)
        for lst, val in zip(outs_s, (fk.reshape(nbs, n_new, fox_heads, HEAD_DIM),
                                     fv.reshape(nbs, n_new, fox_heads, HEAD_DIM),
                                     logf.T.reshape(nbs, n_new, fox_heads),
                                     dk.reshape(nbs, n_new, 2 * diff_heads, HEAD_DIM),
                                     dv.reshape(nbs, n_new, diff_heads, 2 * HEAD_DIM), conv_s)):
            lst.append(val)

    return (xp.reshape(nbp, seq, d), xs.reshape(nbs, n_new, d),
            *[jnp.stack(v) for v in outs_p], *[jnp.stack(v) for v in outs_s])
```

```python
import functools
import math

import jax
import jax.numpy as jnp
from jax import lax
from jax.experimental import pallas as pl
from jax.experimental.pallas import tpu as pltpu

HEAD_DIM = 64
CONV_WIDTH = 3
ROPE_THETA = 10000.0
RMS_EPS = 1e-6
NEG_INF = -1e30
Q_SCALE = HEAD_DIM ** -0.5
LOG2E = math.log2(math.e)

LANES = 128
SUBLANES = 8
V7X_VMEM_BYTES = 64 * 1024 * 1024
VMEM_CAP_BYTES = V7X_VMEM_BYTES - 8 * 1024 * 1024

F32 = jnp.float32
BF16 = jnp.bfloat16

ROW_TILE = 1024
COL_TILE = 1024
ATTN_TILE = 512
ATTN_GROUP = 2
FF_TILE = 1024
FF_SUB = 256
MM_SUB = 256
PAGES_PER_STEP = 8


def _vmem(nbytes):
    return int(min(VMEM_CAP_BYTES, max(32 * 1024 * 1024, 2 * nbytes)))


def _params(n_axes, vmem_bytes):
    return pltpu.CompilerParams(dimension_semantics=("arbitrary",) * n_axes,
                                vmem_limit_bytes=_vmem(vmem_bytes))


def _nt(a, b):
    return lax.dot_general(a, b, (((1,), (1,)), ((), ())), preferred_element_type=F32)


def _nn(a, b):
    return jnp.dot(a, b, preferred_element_type=F32)


def _rms(x, g):
    return x * lax.rsqrt(jnp.mean(x * x, axis=-1, keepdims=True) + RMS_EPS) * g


def _tile(n, pref):
    t = min(n, pref)
    while n % t:
        t //= 2
    return t


def _rmsnorm_kernel(x_ref, g_ref, o_ref):
    o_ref[...] = _rms(x_ref[...], g_ref[...]).astype(o_ref.dtype)


def rmsnorm_bf16(x, g, tm):
    t, d = x.shape
    return pl.pallas_call(
        _rmsnorm_kernel,
        grid=(t // tm,),
        in_specs=[pl.BlockSpec((tm, d), lambda i: (i, 0)),
                  pl.BlockSpec((1, d), lambda i: (0, 0))],
        out_specs=pl.BlockSpec((tm, d), lambda i: (i, 0)),
        out_shape=jax.ShapeDtypeStruct((t, d), BF16),
        compiler_params=_params(1, 2 * tm * d * 6),
        name="rmsnorm_bf16",
    )(x, g.reshape(1, d))


def _resnorm_kernel(x_ref, z_ref, g1_ref, o_ref):
    o_ref[...] = x_ref[...] + _rms(z_ref[...], g1_ref[...])


def residual_norm(x, z, g1, tm):
    t, d = x.shape
    row = pl.BlockSpec((tm, d), lambda i: (i, 0))
    vec = pl.BlockSpec((1, d), lambda i: (0, 0))
    return pl.pallas_call(
        _resnorm_kernel, grid=(t // tm,), in_specs=[row, row, vec], out_specs=row,
        out_shape=jax.ShapeDtypeStruct((t, d), F32),
        compiler_params=_params(1, 2 * tm * d * 12), name="residual_norm",
    )(x, z, g1.reshape(1, d))


def _mm_kernel(*refs, n_extra, n_out, epilogue, w_transposed, feature_major):
    a_ref, w_ref = refs[0], refs[1]
    extra = refs[2:2 + n_extra]
    outs = refs[2 + n_extra:2 + n_extra + n_out]
    wb_ref = refs[-1]

    @pl.when(pl.program_id(1) == 0)
    def _():
        wb_ref[...] = w_ref[...].astype(BF16)

    a = a_ref[...].astype(BF16)
    tn = wb_ref.shape[0] if w_transposed else wb_ref.shape[1]
    sub = _tile(tn, MM_SUB)
    for c0 in range(0, tn, sub):
        if feature_major:
            acc = _nt(wb_ref[c0:c0 + sub, :], a)
        elif w_transposed:
            acc = _nt(a, wb_ref[c0:c0 + sub, :])
        else:
            acc = _nn(a, wb_ref[:, c0:c0 + sub])
        epilogue(acc, extra, outs, c0)


def _epi_plain(scale, feature_major=False):
    def epi(acc, extra, outs, c0):
        val = acc if scale == 1.0 else acc * scale
        for o in outs:
            if feature_major:
                o[0, c0:c0 + acc.shape[0], :] = val.astype(o.dtype)
            else:
                o[:, c0:c0 + acc.shape[1]] = val.astype(o.dtype)
    return epi


def _epi_heads(acc, extra, outs, c0):
    native, dense = outs
    for h in range(acc.shape[1] // LANES):
        native[:, c0 // LANES + h, :] = acc[:, h * LANES:(h + 1) * LANES]
    dense[:, c0:c0 + acc.shape[1]] = acc.astype(dense.dtype)


def _epi_rope(scale):
    def epi(acc, extra, outs, c0):
        cos = extra[0][...]
        sin = extra[1][...]
        first_half = (lax.broadcasted_iota(jnp.int32, (1, LANES), 1) % HEAD_DIM) < HEAD_DIM // 2
        for c in range(acc.shape[1] // LANES):
            x = acc[:, c * LANES:(c + 1) * LANES]
            swapped = jnp.where(first_half, pltpu.roll(x, LANES - HEAD_DIM // 2, 1),
                                pltpu.roll(x, HEAD_DIM // 2, 1))
            val = x * cos + swapped * sin
            if scale != 1.0:
                val = val * scale
            for o in outs:
                o[:, c0 + c * LANES:c0 + (c + 1) * LANES] = val.astype(o.dtype)
    return epi


def _epi_rope_fm(acc, extra, outs, c0):
    cos = extra[0][...]
    sin = extra[1][...]
    half = HEAD_DIM // 2
    for m in range(acc.shape[0] // HEAD_DIM):
        x1 = acc[m * HEAD_DIM:m * HEAD_DIM + half]
        x2 = acc[m * HEAD_DIM + half:(m + 1) * HEAD_DIM]
        y1 = x1 * cos - x2 * sin
        y2 = x2 * cos + x1 * sin
        r0 = c0 + m * HEAD_DIM
        for o in outs:
            o[0, r0:r0 + half, :] = y1.astype(o.dtype)
            o[0, r0 + half:r0 + HEAD_DIM, :] = y2.astype(o.dtype)


def matmul(a, w, *, w_transposed, col_block, n_blocks, tn, tm, outs, epilogue, extras=(),
           feature_major=False, row_offset=0):
    t, k = a.shape
    in_specs = [pl.BlockSpec((tm, k), lambda j, i: (i, 0))]
    if w_transposed:
        in_specs.append(pl.BlockSpec((pl.Element(tn), pl.Element(k)),
                                     lambda j, i: (pl.multiple_of(row_offset + (col_block + j) * tn,
                                                                  SUBLANES), 0)))
        w_scratch = pltpu.VMEM((tn, k), BF16)
    else:
        in_specs.append(pl.BlockSpec((k, tn), lambda j, i: (0, col_block + j)))
        w_scratch = pltpu.VMEM((k, tn), BF16)
    in_specs += [spec for _, spec in extras]
    nbytes = 2 * tm * k * a.dtype.itemsize + 2 * k * tn * 4 + k * tn * 2 + 2 * tm * tn * 4
    nbytes += sum(2 * tm * tn * s.dtype.itemsize for s, _ in outs)
    return pl.pallas_call(
        functools.partial(_mm_kernel, n_extra=len(extras), n_out=len(outs), epilogue=epilogue,
                          w_transposed=w_transposed, feature_major=feature_major),
        grid=(n_blocks, t // tm),
        in_specs=in_specs,
        out_specs=[spec for _, spec in outs],
        out_shape=[s for s, _ in outs],
        scratch_shapes=[w_scratch],
        compiler_params=_params(2, nbytes),
        name="proj_matmul",
    )(a, w, *[x for x, _ in extras])


def _token_out(t, n, tm, tn, dtype):
    return jax.ShapeDtypeStruct((t, n), dtype), pl.BlockSpec((tm, tn), lambda j, i: (i, j))


def _feature_out(nb, n, seq, tm, tn, dtype):
    tps = seq // tm
    return (jax.ShapeDtypeStruct((nb, n, seq), dtype),
            pl.BlockSpec((1, tn, tm), lambda j, i: (i // tps, j, i % tps)))


def _log_sigmoid(x):
    return jnp.minimum(x, 0.0) - jnp.log1p(jnp.exp(-jnp.abs(x)))


def _logf_kernel(a_ref, w_ref, b_ref, o_ref):
    acc = _nt(w_ref[...].astype(BF16), a_ref[...])
    o_ref[0] = _log_sigmoid(acc + b_ref[...])


def forget_gate(a, w_ff_t, b_forget, seq, tm):
    t, k = a.shape
    nh = w_ff_t.shape[0]
    tps = seq // tm
    return pl.pallas_call(
        _logf_kernel, grid=(t // tm,),
        in_specs=[pl.BlockSpec((tm, k), lambda i: (i, 0)),
                  pl.BlockSpec((nh, k), lambda i: (0, 0)),
                  pl.BlockSpec((nh, 1), lambda i: (0, 0))],
        out_specs=pl.BlockSpec((1, nh, tm), lambda i: (i // tps, 0, i % tps)),
        out_shape=jax.ShapeDtypeStruct((t // seq, nh, seq), F32),
        compiler_params=_params(1, 2 * tm * k * 2 + k * LANES * 8),
        name="forget_gate",
    )(a, w_ff_t, b_forget.reshape(nh, 1))


def _scan_lanes(x):
    n = x.shape[1]
    lane = lax.broadcasted_iota(jnp.int32, (1, n), 1)
    shift = 1
    while shift < n:
        x = x + jnp.where(lane >= shift, pltpu.roll(x, shift, 1), 0.0)
        shift *= 2
    return x


def _split3(c):
    hi = c.astype(BF16)
    r1 = c - hi.astype(F32)
    mid = r1.astype(BF16)
    lo = (r1 - mid.astype(F32)).astype(BF16)
    return hi, mid, lo


def _cumsum_prompt_kernel(lf_ref, ct_ref, c_ref, *, blk):
    c = _scan_lanes(lf_ref[0]) * LOG2E
    ct_ref[0] = c
    eye = (lax.broadcasted_iota(jnp.int32, (blk, blk), 0)
           == lax.broadcasted_iota(jnp.int32, (blk, blk), 1)).astype(BF16)
    for j in range(c.shape[1] // blk):
        hi, mid, lo = _split3(c[:, j * blk:(j + 1) * blk])
        c_ref[j * blk:(j + 1) * blk, :] = _nt(eye, hi) + _nt(eye, mid) + _nt(eye, lo)


def cumsum_prompt(logf_t, blk):
    b, nh, s = logf_t.shape
    return pl.pallas_call(
        functools.partial(_cumsum_prompt_kernel, blk=blk), grid=(b,),
        in_specs=[pl.BlockSpec((1, nh, s), lambda i: (i, 0, 0))],
        out_specs=[pl.BlockSpec((1, nh, s), lambda i: (i, 0, 0)),
                   pl.BlockSpec((s, nh), lambda i: (i, 0))],
        out_shape=[jax.ShapeDtypeStruct((b, nh, s), F32), jax.ShapeDtypeStruct((b * s, nh), F32)],
        compiler_params=_params(1, 8 * s * LANES * 4),
        name="cumsum_prompt",
    )(logf_t)


def _cumsum_sample_kernel(pt_ref, lfn_ref, cache_ref, c_ref, buf, sem, *, n_pages, page, n_new):
    b = pl.program_id(0)
    slot = b % 2

    def page_copies(bb, sl):
        return [pltpu.make_async_copy(cache_ref.at[pt_ref[bb, p]], buf.at[sl, :, pl.ds(p * page, page)],
                                      sem.at[sl]) for p in range(n_pages)]

    @pl.when(b == 0)
    def _():
        for cp in page_copies(b, slot):
            cp.start()

    @pl.when(b + 1 < pl.num_programs(0))
    def _():
        for cp in page_copies(b + 1, 1 - slot):
            cp.start()

    n_tok = lfn_ref.shape[1]
    tok = lax.broadcasted_iota(jnp.int32, (n_tok, 1), 0)
    j = lax.broadcasted_iota(jnp.int32, (1, page), 1)
    sel = ((tok == b * n_new + j) & (j < n_new)).astype(BF16)
    hi, mid, lo = _split3(lfn_ref[...])
    buf[slot, :, n_pages * page:] = _nn(hi, sel) + _nn(mid, sel) + _nn(lo, sel)
    for cp in page_copies(b, slot):
        cp.wait()
    c_ref[0] = _scan_lanes(buf[slot])


def cumsum_sample(page_table, logf_new_t, cache_logf_t, n_new):
    nb, n_pages = page_table.shape
    _, nh, page = cache_logf_t.shape
    n_cols = n_pages * page + page
    grid_spec = pltpu.PrefetchScalarGridSpec(
        num_scalar_prefetch=1, grid=(nb,),
        in_specs=[pl.BlockSpec(logf_new_t.shape, lambda i, pt: (0, 0)),
                  pl.BlockSpec(memory_space=pl.ANY)],
        out_specs=pl.BlockSpec((1, nh, n_cols), lambda i, pt: (i, 0, 0)),
        scratch_shapes=[pltpu.VMEM((2, nh, n_cols), F32), pltpu.SemaphoreType.DMA((2,))])
    return pl.pallas_call(
        functools.partial(_cumsum_sample_kernel, n_pages=n_pages, page=page, n_new=n_new),
        grid_spec=grid_spec,
        out_shape=jax.ShapeDtypeStruct((nb, nh, n_cols), F32),
        compiler_params=_params(1, 24 * nh * n_cols * 4),
        name="cumsum_sample",
    )(page_table, logf_new_t, cache_logf_t)


def _flash_pairs(q2s, kv_fns, n_wide, tq, parity, cqs, ck_fns, sum_rows):
    lane = lax.broadcasted_iota(jnp.int32, (1, LANES), 1)
    qms = []
    for q2 in q2s:
        qms += [jnp.where(lane < HEAD_DIM, q2, jnp.zeros_like(q2)),
                jnp.where(lane >= HEAD_DIM, q2, jnp.zeros_like(q2))]

    def block(j, carry, width, masked):
        r0 = pl.multiple_of(j * (2 * tq), 2 * tq)
        out = []
        for n, kv_fn in enumerate(kv_fns):
            kt, pvs = kv_fn(r0, width)
            for a in range(2):
                c = 2 * n + a
                m, l, acc = carry[c]
                s = _nn(qms[c], kt)
                if ck_fns is not None:
                    s = s - ck_fns[c](r0, width)
                if masked:
                    rows = parity * tq + lax.broadcasted_iota(jnp.int32, (tq, width), 0)
                    s = jnp.where(lax.broadcasted_iota(jnp.int32, (tq, width), 1) <= rows, s, NEG_INF)
                row_max = jnp.max(s, axis=1, keepdims=True)
                m_new = jnp.maximum(m, row_max if cqs is None else cqs[c] + row_max)
                p = jnp.exp2(s + ((-m_new) if cqs is None else (cqs[c] - m_new)))
                alpha = jnp.exp2(m - m_new)
                if sum_rows:
                    l = alpha * l + jnp.sum(p, axis=1, keepdims=True)
                acc = alpha * acc + pvs[a](p.astype(BF16))
                out.append((m_new, l, acc))
        return tuple(out)

    init = (jnp.full((tq, 1), -jnp.inf, F32), jnp.zeros((tq, 1), F32), jnp.zeros((tq, LANES), F32))
    carry = lax.fori_loop(0, n_wide, lambda j, c: block(j, c, 2 * tq, False), (init,) * (2 * len(q2s)))
    carry = block(n_wide, carry, (parity + 1) * tq, True)
    return [(acc, l) for _, l, acc in carry]


def _fox_prompt_kernel(q_ref, k_ref, v_ref, c_ref, ct_ref, o_ref, *, tq, n_heads, group):
    j = pl.program_id(1)
    lane = lax.broadcasted_iota(jnp.int32, (1, LANES), 1)
    head = lax.broadcasted_iota(jnp.int32, (1, n_heads), 1)
    feat = lax.broadcasted_iota(jnp.int32, (LANES, 1), 0)

    def make_kv_fn(col):
        def kv_fn(r0, width):
            vt = v_ref[0, col:col + LANES, pl.ds(r0, width)]
            vts = [jnp.where(feat < HEAD_DIM, vt, jnp.ones_like(vt)),
                   jnp.where(feat >= HEAD_DIM, vt, jnp.ones_like(vt))]
            return (k_ref[0, col:col + LANES, pl.ds(r0, width)],
                    [lambda p, v=v: _nt(p, v) for v in vts])
        return kv_fn

    for parity in range(2):
        rows = slice(parity * tq, (parity + 1) * tq)
        c_blk = c_ref[rows, :]
        for g in range(0, n_heads // 2, group):
            slabs = range(g, min(g + group, n_heads // 2))
            hs = [2 * hp + a for hp in slabs for a in range(2)]
            cqs = [jnp.sum(jnp.where(head == h, c_blk, 0.0), axis=1, keepdims=True) for h in hs]
            ck_fns = [lambda r0, width, h=h: ct_ref[0, h:h + 1, pl.ds(r0, width)] for h in hs]
            res = _flash_pairs([q_ref[rows, hp * LANES:(hp + 1) * LANES] for hp in slabs],
                               [make_kv_fn(hp * LANES) for hp in slabs], j, tq, parity, cqs, ck_fns, False)
            for n, hp in enumerate(slabs):
                acc0, acc1 = res[2 * n][0], res[2 * n + 1][0]
                o0 = acc0 / pltpu.roll(acc0, HEAD_DIM, 1)
                o1 = acc1 / pltpu.roll(acc1, HEAD_DIM, 1)
                o_ref[rows, hp * LANES:(hp + 1) * LANES] = jnp.where(lane < HEAD_DIM, o0, o1).astype(o_ref.dtype)


def _lambda(lq1, lk1, lq2, lk2, lam0):
    return (jnp.exp(jnp.sum(lq1 * lk1, axis=1, keepdims=True))
            - jnp.exp(jnp.sum(lq2 * lk2, axis=1, keepdims=True)) + lam0)


def _diff_prompt_kernel(q_ref, k_ref, v_ref, lq1, lk1, lq2, lk2, g_ref, o_ref, *, tq, n_heads, lam0,
                        group):
    j = pl.program_id(1)
    lam = _lambda(lq1[...], lk1[...], lq2[...], lk2[...], lam0)

    def make_kv_fn(col):
        def kv_fn(r0, width):
            vb = v_ref[pl.ds(r0, width), col:col + LANES]
            return k_ref[0, col:col + LANES, pl.ds(r0, width)], [lambda p: _nn(p, vb)] * 2
        return kv_fn

    for parity in range(2):
        rows = slice(parity * tq, (parity + 1) * tq)
        for g in range(0, n_heads, group):
            slabs = range(g, min(g + group, n_heads))
            res = _flash_pairs([q_ref[rows, h * LANES:(h + 1) * LANES] for h in slabs],
                               [make_kv_fn(h * LANES) for h in slabs], j, tq, parity, None, None, True)
            for n, h in enumerate(slabs):
                (acc0, l0), (acc1, l1) = res[2 * n], res[2 * n + 1]
                o = acc0 / l0 - lam * (acc1 / l1)
                o_ref[rows, h * LANES:(h + 1) * LANES] = (_rms(o, g_ref[...]) * (1.0 - lam0)).astype(o_ref.dtype)


def fox_prompt(q, k_t, v_t, c, c_t, tq):
    t, w = q.shape
    nb, _, s = k_t.shape
    nq = s // (2 * tq)
    nh = c.shape[1]
    return pl.pallas_call(
        functools.partial(_fox_prompt_kernel, tq=tq, n_heads=nh, group=ATTN_GROUP),
        grid=(nb, nq),
        in_specs=[pl.BlockSpec((2 * tq, w), lambda b, i: (b * nq + i, 0)),
                  pl.BlockSpec((1, w, s), lambda b, i: (b, 0, 0)),
                  pl.BlockSpec((1, w, s), lambda b, i: (b, 0, 0)),
                  pl.BlockSpec((2 * tq, nh), lambda b, i: (b * nq + i, 0)),
                  pl.BlockSpec((1, nh, s), lambda b, i: (b, 0, 0))],
        out_specs=pl.BlockSpec((2 * tq, w), lambda b, i: (b * nq + i, 0)),
        out_shape=jax.ShapeDtypeStruct((t, w), BF16),
        compiler_params=_params(2, 4 * s * w * 2 + 8 * tq * w * 2 + 40 * tq * tq * 4),
        name="fox_prompt",
    )(q, k_t, v_t, c, c_t)


def diff_prompt(q, k_t, v, lams, gain, lam0, tq):
    t, w = q.shape
    nb, _, s = k_t.shape
    nq = s // (2 * tq)
    vw = v.shape[1]
    vec = pl.BlockSpec((1, HEAD_DIM), lambda b, i: (0, 0))
    return pl.pallas_call(
        functools.partial(_diff_prompt_kernel, tq=tq, n_heads=vw // LANES, lam0=lam0, group=ATTN_GROUP),
        grid=(nb, nq),
        in_specs=[pl.BlockSpec((2 * tq, w), lambda b, i: (b * nq + i, 0)),
                  pl.BlockSpec((1, w, s), lambda b, i: (b, 0, 0)),
                  pl.BlockSpec((s, vw), lambda b, i: (b, 0)),
                  vec, vec, vec, vec,
                  pl.BlockSpec((1, LANES), lambda b, i: (0, 0))],
        out_specs=pl.BlockSpec((2 * tq, vw), lambda b, i: (b * nq + i, 0)),
        out_shape=jax.ShapeDtypeStruct((t, vw), BF16),
        compiler_params=_params(2, 4 * s * w * 2 + 8 * tq * w * 2 + 40 * tq * tq * 4),
        name="diff_prompt",
    )(q, k_t, v, *[x.reshape(1, HEAD_DIM) for x in lams], gain.reshape(1, LANES))


def _sample_attn_kernel(pt_ref, *refs, fox, pages_per_step, n_steps, page, n_new, lam0):
    n_in = 6 if fox else 10
    b = pl.program_id(0)
    s = pl.program_id(1)
    nb = pl.num_programs(0)
    t = b * n_steps + s
    copies, step = _paged_attention(fox, pt_ref, refs[:n_in], refs[n_in], refs[n_in + 1:],
                                    pages_per_step=pages_per_step, n_steps=n_steps, page=page,
                                    n_new=n_new, lam0=lam0)

    @pl.when(t == 0)
    def _():
        _start_all(copies(b, s, t % 2))

    @pl.when(t + 1 < nb * n_steps)
    def _():
        t1 = t + 1
        _start_all(copies(t1 // n_steps, t1 % n_steps, t1 % 2))

    step(b, s, t % 2)


def _start_all(cps):
    for n, cp in enumerate(cps):
        cp.start(priority=n % 2)


def _paged_attention(fox, pt_ref, ins, o_ref, scratch, *, pages_per_step, n_steps, page, n_new, lam0):
    if fox:
        q_ref, kn_ref, vn_ref, c_ref, kc_ref, vc_ref = ins
    else:
        q_ref, kn_ref, vn_ref, lq1, lk1, lq2, lk2, g_ref, kc_ref, vc_ref = ins
    kbuf, vbuf, sem, qbd_ref, m_ref, l_ref, acc_ref, cq_ref = scratch
    w = q_ref.shape[1]
    n_rows = qbd_ref.shape[0]
    n_groups = n_rows // n_new
    n_heads_v = w // LANES
    n_past = n_steps * pages_per_step * page
    chunk = pages_per_step * page

    def copies(bb, ss, sl):
        out = []
        for p in range(pages_per_step):
            pg = pt_ref[bb, ss * pages_per_step + p]
            out.append(pltpu.make_async_copy(kc_ref.at[pg], kbuf.at[sl, :, pl.ds(p * page, page)],
                                             sem.at[sl, 0]))
            if fox:
                vdst = vbuf.at[sl, :, pl.ds(p * page, page)]
            else:
                vdst = vbuf.at[sl, pl.ds(p * page * n_heads_v, page * n_heads_v), :]
            out.append(pltpu.make_async_copy(vc_ref.at[pg], vdst, sem.at[sl, 1]))
        return out

    row = lax.broadcasted_iota(jnp.int32, (n_rows, 1), 0)

    def bias(col0, width):
        cb = c_ref[0, :, pl.ds(col0, width)]
        return jnp.broadcast_to(cb[:, None, :], (n_groups, n_new, width)).reshape(n_rows, width)

    def init():
        qt = jnp.concatenate([q_ref[...]] * n_groups, axis=0)
        colg = lax.broadcasted_iota(jnp.int32, (1, w), 1) // HEAD_DIM
        qbd_ref[...] = jnp.where(row // n_new == colg, qt, 0.0).astype(BF16)
        m_ref[...] = jnp.full(m_ref.shape, -jnp.inf, F32)
        l_ref[...] = jnp.zeros(l_ref.shape, F32)
        acc_ref[...] = jnp.zeros(acc_ref.shape, F32)
        if fox:
            lane = lax.broadcasted_iota(jnp.int32, (1, page), 1)
            cq_ref[...] = jnp.sum(jnp.where(lane == row % n_new, bias(n_past, page), 0.0),
                                  axis=1, keepdims=True)
        else:
            cq_ref[...] = jnp.zeros(cq_ref.shape, F32)

    def update(sc, pv):
        cq = cq_ref[...]
        m_prev = m_ref[...]
        m_new = jnp.maximum(m_prev, cq + jnp.max(sc, axis=1, keepdims=True))
        p = jnp.exp(sc + (cq - m_new))
        alpha = jnp.exp(m_prev - m_new)
        l_ref[...] = alpha * l_ref[...] + jnp.sum(p, axis=1, keepdims=True)
        acc_ref[...] = alpha * acc_ref[...] + pv(p.astype(BF16))
        m_ref[...] = m_new

    def pv_heads(p, v_of_head):
        rows_per_head = 2 * n_new
        return jnp.concatenate(
            [_nn(p[h * rows_per_head:(h + 1) * rows_per_head], v_of_head(h)) for h in range(n_heads_v)],
            axis=0)

    def finish():
        pad = jnp.zeros((page - n_new, w), F32)
        kn = jnp.concatenate([kn_ref[...], pad], axis=0).astype(BF16)
        vn = jnp.concatenate([vn_ref[...], pad], axis=0).astype(BF16)
        sn = _nt(qbd_ref[...], kn)
        if fox:
            sn = sn - bias(n_past, page)
        lane = lax.broadcasted_iota(jnp.int32, (1, page), 1)
        sn = jnp.where(lane <= row % n_new, sn, NEG_INF)
        if fox:
            update(sn, lambda p: _nn(p, vn))
        else:
            update(sn, lambda p: pv_heads(p, lambda h: vn[:, h * LANES:(h + 1) * LANES]))

        on = acc_ref[...] / l_ref[...]
        if fox:
            col = lax.broadcasted_iota(jnp.int32, (1, 1, w), 2)
            grp = lax.broadcasted_iota(jnp.int32, (n_groups, 1, 1), 0)
            o3 = on.reshape(n_groups, n_new, w)
            o_ref[...] = jnp.sum(jnp.where(grp == col // HEAD_DIM, o3, 0.0), axis=0)
        else:
            lam = _lambda(lq1[...], lk1[...], lq2[...], lk2[...], lam0)
            for h in range(n_heads_v):
                r0 = 2 * h * n_new
                oh = on[r0:r0 + n_new] - lam * on[r0 + n_new:r0 + 2 * n_new]
                o_ref[:, h * LANES:(h + 1) * LANES] = _rms(oh, g_ref[...]) * (1.0 - lam0)

    def step(b, s, slot):
        pl.when(s == 0)(init)
        for cp in copies(b, s, slot):
            cp.wait()
        sc = _nn(qbd_ref[...], kbuf[slot].astype(BF16))
        if fox:
            sc = sc - bias(pl.multiple_of(s * chunk, chunk), chunk)
            vt = vbuf[slot].astype(BF16)
            update(sc, lambda p: _nt(p, vt))
        else:
            update(sc, lambda p: pv_heads(
                p, lambda h: vbuf[slot, pl.ds(h, chunk, stride=n_heads_v), :].astype(BF16)))
        pl.when(s == n_steps - 1)(finish)

    return copies, step


def sample_attention(page_table, q, k_new, v_new, k_cache_t, v_cache, pages_per_step, *,
                     c_all=None, lams=None, gain=None, lam0=0.0):
    fox = c_all is not None
    nb, n_pages = page_table.shape
    _, w, page = k_cache_t.shape
    n_new = q.shape[0] // nb
    n_steps = n_pages // pages_per_step
    chunk = pages_per_step * page
    n_rows = (w // HEAD_DIM) * n_new
    new_spec = pl.BlockSpec((n_new, w), lambda b, s, pt: (b, 0))
    any_spec = pl.BlockSpec(memory_space=pl.ANY)
    if fox:
        ins = [q, k_new, v_new, c_all, k_cache_t, v_cache]
        in_specs = [new_spec, new_spec, new_spec,
                    pl.BlockSpec((1,) + c_all.shape[1:], lambda b, s, pt: (b, 0, 0)),
                    any_spec, any_spec]
        v_scratch = pltpu.VMEM((2, w, chunk), F32)
        acc_cols = w
    else:
        vec = pl.BlockSpec((1, HEAD_DIM), lambda b, s, pt: (0, 0))
        ins = ([q, k_new, v_new] + [x.reshape(1, HEAD_DIM) for x in lams]
               + [gain.reshape(1, LANES), k_cache_t, v_cache])
        in_specs = [new_spec, new_spec, new_spec, vec, vec, vec, vec,
                    pl.BlockSpec((1, LANES), lambda b, s, pt: (0, 0)), any_spec, any_spec]
        v_scratch = pltpu.VMEM((2, chunk * (w // LANES), LANES), F32)
        acc_cols = LANES
    grid_spec = pltpu.PrefetchScalarGridSpec(
        num_scalar_prefetch=1, grid=(nb, n_steps), in_specs=in_specs,
        out_specs=new_spec,
        scratch_shapes=[pltpu.VMEM((2, w, chunk), F32), v_scratch,
                        pltpu.SemaphoreType.DMA((2, 2)),
                        pltpu.VMEM((n_rows, w), BF16),
                        pltpu.VMEM((n_rows, 1), F32), pltpu.VMEM((n_rows, 1), F32),
                        pltpu.VMEM((n_rows, acc_cols), F32), pltpu.VMEM((n_rows, 1), F32)])
    nbytes = 4 * chunk * w * 4 + 2 * chunk * w * 2 + 4 * n_rows * chunk * 4
    if fox:
        nbytes += 2 * c_all.shape[1] * c_all.shape[2] * 4
    return pl.pallas_call(
        functools.partial(_sample_attn_kernel, fox=fox, pages_per_step=pages_per_step, n_steps=n_steps,
                          page=page, n_new=n_new, lam0=lam0),
        grid_spec=grid_spec,
        out_shape=jax.ShapeDtypeStruct((nb * n_new, w), F32),
        compiler_params=_params(2, nbytes),
        name="fox_sample" if fox else "diff_sample",
    )(page_table, *ins)


def _merge_kernel(fo_ref, do_ref, wf_ref, wd_ref, gf_ref, gd_ref, o_ref, wfb, wdb):
    @pl.when(pl.program_id(1) == 0)
    def _():
        wfb[...] = wf_ref[0].astype(BF16)
        wdb[...] = wd_ref[0].astype(BF16)

    pf = _nn(fo_ref[...].astype(BF16), wfb[...])
    pd = _nn(do_ref[...].astype(BF16), wdb[...])
    o_ref[...] = (jax.nn.sigmoid(gf_ref[...]) * pf + jax.nn.sigmoid(gd_ref[...]) * pd).astype(o_ref.dtype)


def branch_merge(fo, do, w_branch, gates, tm, tn):
    t, bw = fo.shape
    d = w_branch.shape[2]
    nn = d // tn
    nbytes = 4 * tm * bw * fo.dtype.itemsize + 4 * bw * tn * 4 + 2 * bw * tn * 2 + 6 * tm * tn * 4
    return pl.pallas_call(
        _merge_kernel, grid=(nn, t // tm),
        in_specs=[pl.BlockSpec((tm, bw), lambda j, i: (i, 0)),
                  pl.BlockSpec((tm, bw), lambda j, i: (i, 0)),
                  pl.BlockSpec((1, bw, tn), lambda j, i: (0, 0, j)),
                  pl.BlockSpec((1, bw, tn), lambda j, i: (1, 0, j)),
                  pl.BlockSpec((tm, tn), lambda j, i: (i, j)),
                  pl.BlockSpec((tm, tn), lambda j, i: (i, nn + j))],
        out_specs=pl.BlockSpec((tm, tn), lambda j, i: (i, j)),
        out_shape=jax.ShapeDtypeStruct((t, d), BF16),
        scratch_shapes=[pltpu.VMEM((bw, tn), BF16), pltpu.VMEM((bw, tn), BF16)],
        compiler_params=_params(2, nbytes),
        name="branch_merge",
    )(fo, do, w_branch, w_branch, gates, gates)


def _gelu_tanh(x):
    return 0.5 * x * (1.0 + jnp.tanh(math.sqrt(2.0 / math.pi) * (x + 0.044715 * (x * x * x))))


def _causal_conv(u, cw, prev1, prev2, period):
    n = u.shape[0]
    pos = lax.broadcasted_iota(jnp.int32, (n, 1), 0) % period
    u1 = jnp.where(pos == 0, prev1, pltpu.roll(u, 1, 0))
    u2 = jnp.where(pos == 0, prev2, jnp.where(pos == 1, prev1, pltpu.roll(u, 2, 0)))
    return cw[0:1] * u2 + cw[1:2] * u1 + cw[2:3] * u


def _ffn_prompt_kernel(h_ref, wg_ref, wv_ref, cwg_ref, cwv_ref, wd_ref, f_ref, cs_ref,
                       carry_ref, *, tiles_per_seq, sub):
    i = pl.program_id(0)
    f = pl.program_id(1)
    tm = h_ref.shape[0]
    tf = wg_ref.shape[1]
    h = h_ref[...]

    @pl.when(i % tiles_per_seq == 0)
    def _():
        carry_ref[f] = jnp.zeros(carry_ref.shape[1:], F32)

    @pl.when(f == 0)
    def _():
        f_ref[...] = jnp.zeros(f_ref.shape, F32)

    acts = []
    for c in range(tf // sub):
        cs = slice(c * sub, (c + 1) * sub)
        ug = _nn(h, wg_ref[:, cs])
        uv = _nn(h, wv_ref[:, cs])
        pg = carry_ref[f, 0, :, cs]
        pv = carry_ref[f, 1, :, cs]
        yg = _causal_conv(ug, cwg_ref[:, cs], pg[SUBLANES - 1:SUBLANES], pg[SUBLANES - 2:SUBLANES - 1], tm)
        yv = _causal_conv(uv, cwv_ref[:, cs], pv[SUBLANES - 1:SUBLANES], pv[SUBLANES - 2:SUBLANES - 1], tm)
        carry_ref[f, 0, :, cs] = ug[tm - SUBLANES:tm]
        carry_ref[f, 1, :, cs] = uv[tm - SUBLANES:tm]
        cs_ref[0, 0, :, cs] = ug[tm - (CONV_WIDTH - 1):tm]
        cs_ref[0, 1, :, cs] = uv[tm - (CONV_WIDTH - 1):tm]
        acts.append((_gelu_tanh(yg) * yv).astype(BF16))

    f_ref[...] += _nn(jnp.concatenate(acts, axis=1), wd_ref[...])


def cast_bf16(w):
    n, m = w.shape
    rows = _tile(n, max(SUBLANES, (4 * 1024 * 1024) // (4 * m)))

    def kern(w_ref, o_ref):
        o_ref[...] = w_ref[...].astype(BF16)

    return pl.pallas_call(
        kern, grid=(n // rows,),
        in_specs=[pl.BlockSpec((rows, m), lambda i: (i, 0))],
        out_specs=pl.BlockSpec((rows, m), lambda i: (i, 0)),
        out_shape=jax.ShapeDtypeStruct((n, m), BF16),
        compiler_params=_params(1, 2 * rows * m * 6),
        name="cast_bf16",
    )(w)


def ffn_prompt(h, w_up, conv_w, w_down, seq, tm, tf, sub):
    t, d = h.shape
    dff = w_down.shape[0]
    nf = dff // tf
    tiles_per_seq = seq // tm
    nbytes = (2 * tm * d * 2 + 4 * d * tf * 2 + 2 * tf * d * 2 + 2 * tm * d * 4
              + 10 * tm * tf * 4 + nf * 2 * SUBLANES * tf * 4)
    return pl.pallas_call(
        functools.partial(_ffn_prompt_kernel, tiles_per_seq=tiles_per_seq, sub=sub),
        grid=(t // tm, nf),
        in_specs=[pl.BlockSpec((tm, d), lambda i, f: (i, 0)),
                  pl.BlockSpec((d, tf), lambda i, f: (0, f)),
                  pl.BlockSpec((d, tf), lambda i, f: (0, nf + f)),
                  pl.BlockSpec((CONV_WIDTH, tf), lambda i, f: (0, f)),
                  pl.BlockSpec((CONV_WIDTH, tf), lambda i, f: (0, nf + f)),
                  pl.BlockSpec((tf, d), lambda i, f: (f, 0))],
        out_specs=[pl.BlockSpec((tm, d), lambda i, f: (i, 0)),
                   pl.BlockSpec((1, 2, CONV_WIDTH - 1, tf), lambda i, f: (i, 0, 0, f))],
        out_shape=[jax.ShapeDtypeStruct((t, d), F32),
                   jax.ShapeDtypeStruct((t // tm, 2, CONV_WIDTH - 1, dff), F32)],
        scratch_shapes=[pltpu.VMEM((nf, 2, SUBLANES, tf), F32)],
        compiler_params=_params(2, nbytes),
        name="ffn_prompt",
    )(h, w_up, w_up, conv_w, conv_w, w_down)


def _ffn_sample_kernel(h_ref, wg_ref, wv_ref, cwg_ref, cwv_ref, wd_ref, sg_ref, sv_ref, f_ref, cs_ref,
                       *, n_new):
    f = pl.program_id(0)
    t = h_ref.shape[0]
    nb = t // n_new
    tf = wg_ref.shape[1]
    h = h_ref[...]
    ug = _nn(h, wg_ref[...])
    uv = _nn(h, wv_ref[...])

    def rows_of(state, r):
        return jnp.broadcast_to(state[:, r:r + 1, :], (nb, n_new, tf)).reshape(t, tf)

    sg = sg_ref[...]
    sv = sv_ref[...]
    yg = _causal_conv(ug, cwg_ref[...], rows_of(sg, 1), rows_of(sg, 0), n_new)
    yv = _causal_conv(uv, cwv_ref[...], rows_of(sv, 1), rows_of(sv, 0), n_new)
    keep = CONV_WIDTH - 1
    cs_ref[0] = ug.reshape(nb, n_new, tf)[:, n_new - keep:, :]
    cs_ref[1] = uv.reshape(nb, n_new, tf)[:, n_new - keep:, :]

    act = (_gelu_tanh(yg) * yv).astype(BF16)

    @pl.when(f == 0)
    def _():
        f_ref[...] = jnp.zeros(f_ref.shape, F32)

    f_ref[...] += _nn(act, wd_ref[...])


def ffn_sample(h, w_up, conv_w, w_down, state, n_new, tf):
    t, d = h.shape
    dff = w_down.shape[0]
    nf = dff // tf
    nb = t // n_new
    keep = CONV_WIDTH - 1
    nbytes = (2 * t * d * 2 + 4 * d * tf * 2 + 2 * tf * d * 2 + 2 * t * d * 4
              + 12 * t * tf * 4 + 8 * nb * SUBLANES * tf * 4)
    return pl.pallas_call(
        functools.partial(_ffn_sample_kernel, n_new=n_new),
        grid=(nf,),
        in_specs=[pl.BlockSpec((t, d), lambda f: (0, 0)),
                  pl.BlockSpec((d, tf), lambda f: (0, f)),
                  pl.BlockSpec((d, tf), lambda f: (0, nf + f)),
                  pl.BlockSpec((CONV_WIDTH, tf), lambda f: (0, f)),
                  pl.BlockSpec((CONV_WIDTH, tf), lambda f: (0, nf + f)),
                  pl.BlockSpec((tf, d), lambda f: (f, 0)),
                  pl.BlockSpec((nb, keep, tf), lambda f: (0, 0, f)),
                  pl.BlockSpec((nb, keep, tf), lambda f: (0, 0, nf + f))],
        out_specs=[pl.BlockSpec((t, d), lambda f: (0, 0)),
                   pl.BlockSpec((2, nb, keep, tf), lambda f: (0, 0, 0, f))],
        out_shape=[jax.ShapeDtypeStruct((t, d), F32),
                   jax.ShapeDtypeStruct((2, nb, keep, dff), F32)],
        compiler_params=_params(1, nbytes),
        name="ffn_sample",
    )(h, w_up, w_up, conv_w, conv_w, w_down, state, state)


def _lambda_init(layer_idx):
    return 0.8 - 0.6 * math.exp(-0.3 * layer_idx)


def _rope_angles(pos):
    half = HEAD_DIM // 2
    inv_freq = ROPE_THETA ** (-jnp.arange(half, dtype=F32) / half)
    ang = pos.astype(F32)[:, None] * inv_freq[None, :]
    return jnp.cos(ang), jnp.sin(ang)


def _rope_tables(pos):
    cos, sin = _rope_angles(pos)
    return (jnp.concatenate([cos, cos, cos, cos], axis=1),
            jnp.concatenate([-sin, sin, -sin, sin], axis=1))


def _out_proj_kernel(a_ref, w_ref, x_ref, g1_ref, g2_ref, o_ref, h_ref):
    mix = _nn(a_ref[...], w_ref[...])
    y = x_ref[...] + _rms(mix, g1_ref[...])
    o_ref[...] = y
    h_ref[...] = _rms(y, g2_ref[...]).astype(h_ref.dtype)


def out_proj_norm(merged, w_out_b, x, g1, g2, tm):
    t, d = x.shape
    row = pl.BlockSpec((tm, d), lambda i: (i, 0))
    vec = pl.BlockSpec((1, d), lambda i: (0, 0))
    return pl.pallas_call(
        _out_proj_kernel, grid=(t // tm,),
        in_specs=[row, pl.BlockSpec((d, d), lambda i: (0, 0)), row, vec, vec],
        out_specs=[row, row],
        out_shape=[jax.ShapeDtypeStruct((t, d), F32), jax.ShapeDtypeStruct((t, d), BF16)],
        compiler_params=_params(1, 2 * d * d * 2 + 2 * tm * d * (2 + 4 + 4 + 2) + 2 * tm * d * 4),
        name="out_proj_norm",
    )(merged, w_out_b, x, g1.reshape(1, d), g2.reshape(1, d))


def _mixer_tail(x, fo, do, gates, w_branch_l, w_out_b, g_post, g_ffn_pre, tm, tn):
    merged = branch_merge(fo, do, w_branch_l, gates, tm, tn)
    return out_proj_norm(merged, w_out_b, x, g_post, g_ffn_pre, _tile(x.shape[0], 512))


def kernel(x_prompt, x_sample, cache_fox_k, cache_fox_v, cache_fox_logf, cache_diff_k, cache_diff_v,
           state_ffn_conv, page_table, norm_mix_pre, norm_mix_post, w_in, b_forget, lam_q1, lam_k1,
           lam_q2, lam_k2, diff_subln, w_branch, w_out, norm_ffn_pre, norm_ffn_post, w_up, conv_ffn,
           w_down):
    nbp, seq, d = x_prompt.shape
    nbs, n_new, _ = x_sample.shape
    depth, n_pool, page, fox_heads, _ = cache_fox_k.shape
    diff_heads = cache_diff_v.shape[3]
    fw = fox_heads * HEAD_DIM
    dqk = 2 * diff_heads * HEAD_DIM
    dvw = diff_heads * 2 * HEAD_DIM
    dff = w_down.shape[1]
    n_pages = page_table.shape[1]
    n_past = n_pages * page
    tp, ts = nbp * seq, nbs * n_new

    tm = _tile(seq, ROW_TILE)
    tq = _tile(seq, ATTN_TILE)
    tf = _tile(dff, FF_TILE)
    tn = _tile(fw, COL_TILE)
    tn_d = _tile(d, COL_TILE)
    pages_per_step = _tile(n_pages, PAGES_PER_STEP)
    assert dqk == fw and dvw == fw and fw % tn == 0 and (2 * d) % tn == 0

    tps = seq // tm
    cos_p, sin_p = _rope_tables(jnp.arange(seq))
    cos_pt, sin_pt = [x.T for x in _rope_angles(jnp.arange(seq))]
    cos_s, sin_s = _rope_tables(n_past + jnp.tile(jnp.arange(n_new), nbs))

    def rope_extras(cos, sin, rows, period):
        spec = pl.BlockSpec((rows, LANES), lambda j, i: (i % period, 0))
        return [(cos, spec), (sin, spec)]

    xp = x_prompt.reshape(tp, d)
    xs = x_sample.reshape(ts, d)
    outs_p = [[] for _ in range(6)]
    outs_s = [[] for _ in range(6)]
    nblk = fw // tn
    for l in range(depth):
        lam0 = _lambda_init(l)
        lams = (lam_q1[l], lam_k1[l], lam_q2[l], lam_k2[l])
        w_t = w_in[l].T
        w_ff_t = w_t[3 * fw:3 * fw + fox_heads]
        rest_row0 = 3 * fw + fox_heads
        proj = functools.partial(matmul, w_transposed=True, n_blocks=nblk, tn=tn)

        hp = rmsnorm_bf16(xp, norm_mix_pre[l], tm)
        pj = functools.partial(proj, hp, tm=tm)
        fm_pair = [_feature_out(nbp, fw, seq, tm, tn, F32), _feature_out(nbp, fw, seq, tm, tn, BF16)]
        fq = pj(w_t, col_block=0, outs=[_token_out(tp, fw, tm, tn, BF16)],
                epilogue=_epi_plain(Q_SCALE * LOG2E))[0]
        fk, fkb = pj(w_t, col_block=nblk, outs=fm_pair, epilogue=_epi_plain(1.0, True), feature_major=True)
        fv, fvb = pj(w_t, col_block=2 * nblk, outs=fm_pair, epilogue=_epi_plain(1.0, True),
                     feature_major=True)
        logf = forget_gate(hp, w_ff_t, b_forget[l], seq, tm)
        dq = pj(w_t, row_offset=rest_row0,col_block=0, outs=[_token_out(tp, dqk, tm, tn, BF16)],
                epilogue=_epi_rope(Q_SCALE * LOG2E), extras=rope_extras(cos_p, sin_p, tm, tps))[0]
        fm_spec = pl.BlockSpec((HEAD_DIM // 2, tm), lambda j, i: (0, i % tps))
        dk, dkb = pj(w_t, row_offset=rest_row0,col_block=nblk, outs=fm_pair, epilogue=_epi_rope_fm,
                     extras=[(cos_pt, fm_spec), (sin_pt, fm_spec)], feature_major=True)
        dv, dvb = pj(w_t, row_offset=rest_row0,col_block=2 * nblk,
                     outs=[(jax.ShapeDtypeStruct((tp, dvw // LANES, LANES), F32),
                            pl.BlockSpec((tm, tn // LANES, LANES), lambda j, i: (i, j, 0))),
                           _token_out(tp, dvw, tm, tn, BF16)],
                     epilogue=_epi_heads)
        gates = matmul(hp, w_t, row_offset=rest_row0, w_transposed=True, col_block=3 * nblk, n_blocks=2 * d // tn, tn=tn,
                       tm=tm, outs=[_token_out(tp, 2 * d, tm, tn, F32)], epilogue=_epi_plain(1.0))[0]
        c_t, c = cumsum_prompt(logf, tq)
        fo = fox_prompt(fq, fkb, fvb, c, c_t, tq)
        do = diff_prompt(dq, dkb, dvb, lams, diff_subln[l], lam0, tq)
        w_out_b = cast_bf16(w_out[l])
        x2, h2 = _mixer_tail(xp, fo, do, gates, w_branch[l], w_out_b, norm_mix_post[l], norm_ffn_pre[l],
                             tm, tn_d)
        w_up_b = cast_bf16(w_up[l])
        w_down_b = cast_bf16(w_down[l])
        f, conv_p = ffn_prompt(h2, w_up_b, conv_ffn[l], w_down_b, seq, tm, tf, _tile(tf, FF_SUB))
        xp = residual_norm(x2, f, norm_ffn_post[l], _tile(tp, 512))
        conv_p = conv_p[tps - 1::tps]
        conv_p = jnp.transpose(conv_p, (0, 2, 1, 3)).reshape(nbp, CONV_WIDTH - 1, 2 * dff)

        def heads_last(x_t, n_heads):
            return jnp.transpose(x_t.reshape(nbp, n_heads, HEAD_DIM, seq), (0, 3, 1, 2))

        for lst, val in zip(outs_p, (heads_last(fk, fox_heads), heads_last(fv, fox_heads),
                                     jnp.transpose(logf, (0, 2, 1)),
                                     heads_last(dk, 2 * diff_heads),
                                     dv.reshape(nbp, seq, diff_heads, 2 * HEAD_DIM), conv_p)):
            lst.append(val)

        hs = rmsnorm_bf16(xs, norm_mix_pre[l], ts)
        pj = functools.partial(proj, hs, tm=ts)
        tok = lambda n: [_token_out(ts, n, ts, tn, F32)]
        fq = pj(w_t, col_block=0, outs=tok(fw), epilogue=_epi_plain(Q_SCALE))[0]
        fk = pj(w_t, col_block=nblk, outs=tok(fw), epilogue=_epi_plain(1.0))[0]
        fv = pj(w_t, col_block=2 * nblk, outs=tok(fw), epilogue=_epi_plain(1.0))[0]
        logf = forget_gate(hs, w_ff_t, b_forget[l], ts, ts)[0]
        dq = pj(w_t, row_offset=rest_row0,col_block=0, outs=tok(dqk), epilogue=_epi_rope(Q_SCALE),
                extras=rope_extras(cos_s, sin_s, ts, 1))[0]
        dk = pj(w_t, row_offset=rest_row0,col_block=nblk, outs=tok(dqk), epilogue=_epi_rope(1.0),
                extras=rope_extras(cos_s, sin_s, ts, 1))[0]
        dv = pj(w_t, row_offset=rest_row0,col_block=2 * nblk, outs=tok(dvw), epilogue=_epi_plain(1.0))[0]
        gates = matmul(hs, w_t, row_offset=rest_row0, w_transposed=True, col_block=3 * nblk, n_blocks=2 * d // tn, tn=tn,
                       tm=ts, outs=[_token_out(ts, 2 * d, ts, tn, F32)], epilogue=_epi_plain(1.0))[0]
        feature_major = lambda cache: jnp.transpose(cache, (0, 2, 3, 1)).reshape(n_pool, -1, page)
        c_all = cumsum_sample(page_table, logf, jnp.transpose(cache_fox_logf[l], (0, 2, 1)), n_new)
        fo = sample_attention(page_table, fq, fk, fv, feature_major(cache_fox_k[l]),
                              feature_major(cache_fox_v[l]), pages_per_step, c_all=c_all)
        do = sample_attention(page_table, dq, dk, dv, feature_major(cache_diff_k[l]),
                              cache_diff_v[l].reshape(n_pool, page * diff_heads, 2 * HEAD_DIM),
                              pages_per_step, lams=lams, gain=diff_subln[l], lam0=lam0)
        x2, h2 = _mixer_tail(xs, fo, do, gates, w_branch[l], w_out_b, norm_mix_post[l], norm_ffn_pre[l],
                             ts, tn_d)
        f, conv_s = ffn_sample(h2, w_up_b, conv_ffn[l], w_down_b, state_ffn_conv[l], n_new, tf)
        xs = residual_norm(x2, f, norm_ffn_post[l], ts)
        conv_s = jnp.transpose(conv_s, (1, 2, 0, 3)).reshape(nbs, CONV_WIDTH - 1, 2 * dff)
        for lst, val in zip(outs_s, (fk.reshape(nbs, n_new, fox_heads, HEAD_DIM),
                                     fv.reshape(nbs, n_new, fox_heads, HEAD_DIM),
                                     logf.T.reshape(nbs, n_new, fox_heads),
                                     dk.reshape(nbs, n_new, 2 * diff_heads, HEAD_DIM),
                                     dv.reshape(nbs, n_new, diff_heads, 2 * HEAD_DIM), conv_s)):
            lst.append(val)

    return (xp.reshape(nbp, seq, d), xs.reshape(nbs, n_new, d),
            *[jnp.stack(v) for v in outs_p], *[jnp.stack(v) for v in outs_s])
```

```python
import functools
import math

import jax
import jax.numpy as jnp
from jax import lax
from jax.experimental import pallas as pl
from jax.experimental.pallas import tpu as pltpu

HEAD_DIM = 64
CONV_WIDTH = 3
ROPE_THETA = 10000.0
RMS_EPS = 1e-6
NEG_INF = -1e30
Q_SCALE = HEAD_DIM ** -0.5
LOG2E = math.log2(math.e)

LANES = 128
SUBLANES = 8
V7X_VMEM_BYTES = 64 * 1024 * 1024
VMEM_CAP_BYTES = V7X_VMEM_BYTES - 8 * 1024 * 1024

F32 = jnp.float32
BF16 = jnp.bfloat16

ROW_TILE = 1024
COL_TILE = 1024
ATTN_TILE = 512
ATTN_GROUP = 2
FF_TILE = 1024
FF_SUB = 256
MM_SUB = 256
PAGES_PER_STEP = 8


def _vmem(nbytes):
    return int(min(VMEM_CAP_BYTES, max(32 * 1024 * 1024, 2 * nbytes)))


def _params(n_axes, vmem_bytes):
    return pltpu.CompilerParams(dimension_semantics=("arbitrary",) * n_axes,
                                vmem_limit_bytes=_vmem(vmem_bytes))


def _nt(a, b):
    return lax.dot_general(a, b, (((1,), (1,)), ((), ())), preferred_element_type=F32)


def _nn(a, b):
    return jnp.dot(a, b, preferred_element_type=F32)


def _rms(x, g):
    return x * lax.rsqrt(jnp.mean(x * x, axis=-1, keepdims=True) + RMS_EPS) * g


def _tile(n, pref):
    t = min(n, pref)
    while n % t:
        t //= 2
    return t


def _resnorm_kernel(x_ref, z_ref, g1_ref, o_ref):
    o_ref[...] = x_ref[...] + _rms(z_ref[...], g1_ref[...])


def residual_norm(x, z, g1, tm):
    t, d = x.shape
    row = pl.BlockSpec((tm, d), lambda i: (i, 0))
    vec = pl.BlockSpec((1, d), lambda i: (0, 0))
    return pl.pallas_call(
        _resnorm_kernel, grid=(t // tm,), in_specs=[row, row, vec], out_specs=row,
        out_shape=jax.ShapeDtypeStruct((t, d), F32),
        compiler_params=_params(1, 2 * tm * d * 12), name="residual_norm",
    )(x, z, g1.reshape(1, d))


def _mm_kernel(*refs, n_extra, n_out, epilogue, w_transposed, feature_major):
    a_ref, w_ref = refs[0], refs[1]
    extra = refs[2:2 + n_extra]
    outs = refs[2 + n_extra:2 + n_extra + n_out]
    wb_ref = refs[-1]

    @pl.when(pl.program_id(1) == 0)
    def _():
        wb_ref[...] = w_ref[...].astype(BF16)

    a = a_ref[...].astype(BF16)
    tn = wb_ref.shape[0] if w_transposed else wb_ref.shape[1]
    sub = _tile(tn, MM_SUB)
    for c0 in range(0, tn, sub):
        if feature_major:
            acc = _nt(wb_ref[c0:c0 + sub, :], a)
        elif w_transposed:
            acc = _nt(a, wb_ref[c0:c0 + sub, :])
        else:
            acc = _nn(a, wb_ref[:, c0:c0 + sub])
        epilogue(acc, extra, outs, c0)


def _epi_plain(scale, feature_major=False):
    def epi(acc, extra, outs, c0):
        val = acc if scale == 1.0 else acc * scale
        for o in outs:
            if feature_major:
                o[0, c0:c0 + acc.shape[0], :] = val.astype(o.dtype)
            else:
                o[:, c0:c0 + acc.shape[1]] = val.astype(o.dtype)
    return epi


def _epi_heads(acc, extra, outs, c0):
    native, dense = outs
    for h in range(acc.shape[1] // LANES):
        native[:, c0 // LANES + h, :] = acc[:, h * LANES:(h + 1) * LANES]
    dense[:, c0:c0 + acc.shape[1]] = acc.astype(dense.dtype)


def _epi_rope(scale):
    def epi(acc, extra, outs, c0):
        cos = extra[0][...]
        sin = extra[1][...]
        first_half = (lax.broadcasted_iota(jnp.int32, (1, LANES), 1) % HEAD_DIM) < HEAD_DIM // 2
        for c in range(acc.shape[1] // LANES):
            x = acc[:, c * LANES:(c + 1) * LANES]
            swapped = jnp.where(first_half, pltpu.roll(x, LANES - HEAD_DIM // 2, 1),
                                pltpu.roll(x, HEAD_DIM // 2, 1))
            val = x * cos + swapped * sin
            if scale != 1.0:
                val = val * scale
            for o in outs:
                o[:, c0 + c * LANES:c0 + (c + 1) * LANES] = val.astype(o.dtype)
    return epi


def _epi_rope_fm(acc, extra, outs, c0):
    cos = extra[0][...]
    sin = extra[1][...]
    half = HEAD_DIM // 2
    for m in range(acc.shape[0] // HEAD_DIM):
        x1 = acc[m * HEAD_DIM:m * HEAD_DIM + half]
        x2 = acc[m * HEAD_DIM + half:(m + 1) * HEAD_DIM]
        y1 = x1 * cos - x2 * sin
        y2 = x2 * cos + x1 * sin
        r0 = c0 + m * HEAD_DIM
        for o in outs:
            o[0, r0:r0 + half, :] = y1.astype(o.dtype)
            o[0, r0 + half:r0 + HEAD_DIM, :] = y2.astype(o.dtype)


def matmul(a, w, *, w_transposed, col_block, n_blocks, tn, tm, outs, epilogue, extras=(),
           feature_major=False, row_offset=0):
    t, k = a.shape
    in_specs = [pl.BlockSpec((tm, k), lambda j, i: (i, 0))]
    if w_transposed:
        in_specs.append(pl.BlockSpec((pl.Element(tn), pl.Element(k)),
                                     lambda j, i: (pl.multiple_of(row_offset + (col_block + j) * tn,
                                                                  SUBLANES), 0)))
        w_scratch = pltpu.VMEM((tn, k), BF16)
    else:
        in_specs.append(pl.BlockSpec((k, tn), lambda j, i: (0, col_block + j)))
        w_scratch = pltpu.VMEM((k, tn), BF16)
    in_specs += [spec for _, spec in extras]
    nbytes = 2 * tm * k * a.dtype.itemsize + 2 * k * tn * 4 + k * tn * 2 + 2 * tm * tn * 4
    nbytes += sum(2 * tm * tn * s.dtype.itemsize for s, _ in outs)
    return pl.pallas_call(
        functools.partial(_mm_kernel, n_extra=len(extras), n_out=len(outs), epilogue=epilogue,
                          w_transposed=w_transposed, feature_major=feature_major),
        grid=(n_blocks, t // tm),
        in_specs=in_specs,
        out_specs=[spec for _, spec in outs],
        out_shape=[s for s, _ in outs],
        scratch_shapes=[w_scratch],
        compiler_params=_params(2, nbytes),
        name="proj_matmul",
    )(a, w, *[x for x, _ in extras])


def _token_out(t, n, tm, tn, dtype):
    return jax.ShapeDtypeStruct((t, n), dtype), pl.BlockSpec((tm, tn), lambda j, i: (i, j))


def _feature_out(nb, n, seq, tm, tn, dtype):
    tps = seq // tm
    return (jax.ShapeDtypeStruct((nb, n, seq), dtype),
            pl.BlockSpec((1, tn, tm), lambda j, i: (i // tps, j, i % tps)))


def _log_sigmoid(x):
    return jnp.minimum(x, 0.0) - jnp.log1p(jnp.exp(-jnp.abs(x)))


def _norm_logf_kernel(x_ref, g_ref, w_ref, b_ref, h_ref, o_ref):
    h = _rms(x_ref[...], g_ref[...]).astype(BF16)
    h_ref[...] = h
    acc = _nt(w_ref[...].astype(BF16), h)
    o_ref[0] = _log_sigmoid(acc + b_ref[...])


def norm_forget_gate(x, g, w_ff_t, b_forget, seq, tm):
    t, k = x.shape
    nh = w_ff_t.shape[0]
    tps = seq // tm
    return pl.pallas_call(
        _norm_logf_kernel, grid=(t // tm,),
        in_specs=[pl.BlockSpec((tm, k), lambda i: (i, 0)),
                  pl.BlockSpec((1, k), lambda i: (0, 0)),
                  pl.BlockSpec((nh, k), lambda i: (0, 0)),
                  pl.BlockSpec((nh, 1), lambda i: (0, 0))],
        out_specs=[pl.BlockSpec((tm, k), lambda i: (i, 0)),
                   pl.BlockSpec((1, nh, tm), lambda i: (i // tps, 0, i % tps))],
        out_shape=[jax.ShapeDtypeStruct((t, k), BF16), jax.ShapeDtypeStruct((t // seq, nh, seq), F32)],
        compiler_params=_params(1, 2 * tm * k * 6 + tm * k * 4 + k * LANES * 8),
        name="norm_forget_gate",
    )(x, g.reshape(1, k), w_ff_t, b_forget.reshape(nh, 1))


def _scan_lanes(x):
    n = x.shape[1]
    lane = lax.broadcasted_iota(jnp.int32, (1, n), 1)
    shift = 1
    while shift < n:
        x = x + jnp.where(lane >= shift, pltpu.roll(x, shift, 1), 0.0)
        shift *= 2
    return x


def _split3(c):
    hi = c.astype(BF16)
    r1 = c - hi.astype(F32)
    mid = r1.astype(BF16)
    lo = (r1 - mid.astype(F32)).astype(BF16)
    return hi, mid, lo


def _cumsum_prompt_kernel(lf_ref, ct_ref, c_ref, *, blk):
    c = _scan_lanes(lf_ref[0]) * LOG2E
    ct_ref[0] = c
    eye = (lax.broadcasted_iota(jnp.int32, (blk, blk), 0)
           == lax.broadcasted_iota(jnp.int32, (blk, blk), 1)).astype(BF16)
    for j in range(c.shape[1] // blk):
        hi, mid, lo = _split3(c[:, j * blk:(j + 1) * blk])
        c_ref[j * blk:(j + 1) * blk, :] = _nt(eye, hi) + _nt(eye, mid) + _nt(eye, lo)


def cumsum_prompt(logf_t, blk):
    b, nh, s = logf_t.shape
    return pl.pallas_call(
        functools.partial(_cumsum_prompt_kernel, blk=blk), grid=(b,),
        in_specs=[pl.BlockSpec((1, nh, s), lambda i: (i, 0, 0))],
        out_specs=[pl.BlockSpec((1, nh, s), lambda i: (i, 0, 0)),
                   pl.BlockSpec((s, nh), lambda i: (i, 0))],
        out_shape=[jax.ShapeDtypeStruct((b, nh, s), F32), jax.ShapeDtypeStruct((b * s, nh), F32)],
        compiler_params=_params(1, 8 * s * LANES * 4),
        name="cumsum_prompt",
    )(logf_t)


def _cumsum_sample_kernel(pt_ref, lfn_ref, cache_ref, c_ref, buf, sem, *, n_pages, page, n_new):
    b = pl.program_id(0)
    slot = b % 2

    def page_copies(bb, sl):
        return [pltpu.make_async_copy(cache_ref.at[pt_ref[bb, p]], buf.at[sl, :, pl.ds(p * page, page)],
                                      sem.at[sl]) for p in range(n_pages)]

    @pl.when(b == 0)
    def _():
        for cp in page_copies(b, slot):
            cp.start()

    @pl.when(b + 1 < pl.num_programs(0))
    def _():
        for cp in page_copies(b + 1, 1 - slot):
            cp.start()

    n_tok = lfn_ref.shape[1]
    tok = lax.broadcasted_iota(jnp.int32, (n_tok, 1), 0)
    j = lax.broadcasted_iota(jnp.int32, (1, page), 1)
    sel = ((tok == b * n_new + j) & (j < n_new)).astype(BF16)
    hi, mid, lo = _split3(lfn_ref[...])
    buf[slot, :, n_pages * page:] = _nn(hi, sel) + _nn(mid, sel) + _nn(lo, sel)
    for cp in page_copies(b, slot):
        cp.wait()
    c_ref[0] = _scan_lanes(buf[slot])


def cumsum_sample(page_table, logf_new_t, cache_logf_t, n_new):
    nb, n_pages = page_table.shape
    _, nh, page = cache_logf_t.shape
    n_cols = n_pages * page + page
    grid_spec = pltpu.PrefetchScalarGridSpec(
        num_scalar_prefetch=1, grid=(nb,),
        in_specs=[pl.BlockSpec(logf_new_t.shape, lambda i, pt: (0, 0)),
                  pl.BlockSpec(memory_space=pl.ANY)],
        out_specs=pl.BlockSpec((1, nh, n_cols), lambda i, pt: (i, 0, 0)),
        scratch_shapes=[pltpu.VMEM((2, nh, n_cols), F32), pltpu.SemaphoreType.DMA((2,))])
    return pl.pallas_call(
        functools.partial(_cumsum_sample_kernel, n_pages=n_pages, page=page, n_new=n_new),
        grid_spec=grid_spec,
        out_shape=jax.ShapeDtypeStruct((nb, nh, n_cols), F32),
        compiler_params=_params(1, 24 * nh * n_cols * 4),
        name="cumsum_sample",
    )(page_table, logf_new_t, cache_logf_t)


def _flash_pairs(q2s, kv_fns, i, tq, cqs, ck_fns, sum_rows):
    lane = lax.broadcasted_iota(jnp.int32, (1, LANES), 1)
    qms = []
    for q2 in q2s:
        qms += [jnp.where(lane < HEAD_DIM, q2, jnp.zeros_like(q2)),
                jnp.where(lane >= HEAD_DIM, q2, jnp.zeros_like(q2))]

    rows = lax.broadcasted_iota(jnp.int32, (tq, tq), 0)
    cols = lax.broadcasted_iota(jnp.int32, (tq, tq), 1)

    def block(j, carry, diagonal):
        r0 = pl.multiple_of(j * tq, tq)
        out = []
        for n, kv_fn in enumerate(kv_fns):
            kt, pvs = kv_fn(r0)
            for a in range(2):
                c = 2 * n + a
                m, l, acc = carry[c]
                s = _nn(qms[c], kt)
                if ck_fns is not None:
                    s = s - ck_fns[c](r0)
                if diagonal:
                    s = jnp.where(cols <= rows, s, NEG_INF)
                row_max = jnp.max(s, axis=1, keepdims=True)
                m_new = jnp.maximum(m, row_max if cqs is None else cqs[c] + row_max)
                p = jnp.exp2(s + ((-m_new) if cqs is None else (cqs[c] - m_new)))
                alpha = jnp.exp2(m - m_new)
                if sum_rows:
                    l = alpha * l + jnp.sum(p, axis=1, keepdims=True)
                acc = alpha * acc + pvs[a](p.astype(BF16))
                out.append((m_new, l, acc))
        return tuple(out)

    init = (jnp.full((tq, 1), -jnp.inf, F32), jnp.zeros((tq, 1), F32), jnp.zeros((tq, LANES), F32))
    carry = lax.fori_loop(0, i, lambda j, c: block(j, c, False), (init,) * (2 * len(q2s)))
    carry = block(i, carry, True)
    return [(acc, l) for _, l, acc in carry]


def _fox_prompt_kernel(q_ref, k_ref, v_ref, c_ref, ct_ref, o_ref, *, tq, n_heads, group):
    i = pl.program_id(1)
    lane = lax.broadcasted_iota(jnp.int32, (1, LANES), 1)
    head = lax.broadcasted_iota(jnp.int32, (1, n_heads), 1)
    feat = lax.broadcasted_iota(jnp.int32, (LANES, 1), 0)
    c_blk = c_ref[...]

    def make_kv_fn(col):
        def kv_fn(r0):
            vt = v_ref[0, col:col + LANES, pl.ds(r0, tq)]
            vts = [jnp.where(feat < HEAD_DIM, vt, jnp.ones_like(vt)),
                   jnp.where(feat >= HEAD_DIM, vt, jnp.ones_like(vt))]
            return (k_ref[0, col:col + LANES, pl.ds(r0, tq)],
                    [lambda p, v=v: _nt(p, v) for v in vts])
        return kv_fn

    for g in range(0, n_heads // 2, group):
        slabs = range(g, min(g + group, n_heads // 2))
        hs = [2 * hp + a for hp in slabs for a in range(2)]
        cqs = [jnp.sum(jnp.where(head == h, c_blk, 0.0), axis=1, keepdims=True) for h in hs]
        ck_fns = [lambda r0, h=h: ct_ref[0, h:h + 1, pl.ds(r0, tq)] for h in hs]
        res = _flash_pairs([q_ref[:, hp * LANES:(hp + 1) * LANES] for hp in slabs],
                           [make_kv_fn(hp * LANES) for hp in slabs], i, tq, cqs, ck_fns, False)
        for n, hp in enumerate(slabs):
            acc0, acc1 = res[2 * n][0], res[2 * n + 1][0]
            o0 = acc0 / pltpu.roll(acc0, HEAD_DIM, 1)
            o1 = acc1 / pltpu.roll(acc1, HEAD_DIM, 1)
            o_ref[:, hp * LANES:(hp + 1) * LANES] = jnp.where(lane < HEAD_DIM, o0, o1).astype(o_ref.dtype)


def _lambda(lq1, lk1, lq2, lk2, lam0):
    return (jnp.exp(jnp.sum(lq1 * lk1, axis=1, keepdims=True))
            - jnp.exp(jnp.sum(lq2 * lk2, axis=1, keepdims=True)) + lam0)


def _diff_prompt_kernel(q_ref, k_ref, v_ref, lq1, lk1, lq2, lk2, g_ref, o_ref, *, tq, n_heads, lam0,
                        group):
    i = pl.program_id(1)
    lam = _lambda(lq1[...], lk1[...], lq2[...], lk2[...], lam0)

    def make_kv_fn(col):
        def kv_fn(r0):
            vb = v_ref[pl.ds(r0, tq), col:col + LANES]
            return k_ref[0, col:col + LANES, pl.ds(r0, tq)], [lambda p: _nn(p, vb)] * 2
        return kv_fn

    for g in range(0, n_heads, group):
        slabs = range(g, min(g + group, n_heads))
        res = _flash_pairs([q_ref[:, h * LANES:(h + 1) * LANES] for h in slabs],
                           [make_kv_fn(h * LANES) for h in slabs], i, tq, None, None, True)
        for n, h in enumerate(slabs):
            (acc0, l0), (acc1, l1) = res[2 * n], res[2 * n + 1]
            o = acc0 / l0 - lam * (acc1 / l1)
            o_ref[:, h * LANES:(h + 1) * LANES] = (_rms(o, g_ref[...]) * (1.0 - lam0)).astype(o_ref.dtype)


def fox_prompt(q, k_t, v_t, c, c_t, tq):
    t, w = q.shape
    nb, _, s = k_t.shape
    nq = s // tq
    nh = c.shape[1]
    return pl.pallas_call(
        functools.partial(_fox_prompt_kernel, tq=tq, n_heads=nh, group=ATTN_GROUP),
        grid=(nb, nq),
        in_specs=[pl.BlockSpec((tq, w), lambda b, i: (b * nq + i, 0)),
                  pl.BlockSpec((1, w, s), lambda b, i: (b, 0, 0)),
                  pl.BlockSpec((1, w, s), lambda b, i: (b, 0, 0)),
                  pl.BlockSpec((tq, nh), lambda b, i: (b * nq + i, 0)),
                  pl.BlockSpec((1, nh, s), lambda b, i: (b, 0, 0))],
        out_specs=pl.BlockSpec((tq, w), lambda b, i: (b * nq + i, 0)),
        out_shape=jax.ShapeDtypeStruct((t, w), BF16),
        compiler_params=_params(2, 4 * s * w * 2 + 4 * tq * w * 2 + 16 * tq * tq * 4),
        name="fox_prompt",
    )(q, k_t, v_t, c, c_t)


def diff_prompt(q, k_t, v, lams, gain, lam0, tq):
    t, w = q.shape
    nb, _, s = k_t.shape
    nq = s // tq
    vw = v.shape[1]
    vec = pl.BlockSpec((1, HEAD_DIM), lambda b, i: (0, 0))
    return pl.pallas_call(
        functools.partial(_diff_prompt_kernel, tq=tq, n_heads=vw // LANES, lam0=lam0, group=ATTN_GROUP),
        grid=(nb, nq),
        in_specs=[pl.BlockSpec((tq, w), lambda b, i: (b * nq + i, 0)),
                  pl.BlockSpec((1, w, s), lambda b, i: (b, 0, 0)),
                  pl.BlockSpec((s, vw), lambda b, i: (b, 0)),
                  vec, vec, vec, vec,
                  pl.BlockSpec((1, LANES), lambda b, i: (0, 0))],
        out_specs=pl.BlockSpec((tq, vw), lambda b, i: (b * nq + i, 0)),
        out_shape=jax.ShapeDtypeStruct((t, vw), BF16),
        compiler_params=_params(2, 4 * s * w * 2 + 4 * tq * w * 2 + 16 * tq * tq * 4),
        name="diff_prompt",
    )(q, k_t, v, *[x.reshape(1, HEAD_DIM) for x in lams], gain.reshape(1, LANES))


def _sample_attn_kernel(pt_ref, *refs, fox, pages_per_step, n_steps, page, n_new, lam0):
    n_in = 6 if fox else 10
    b = pl.program_id(0)
    s = pl.program_id(1)
    nb = pl.num_programs(0)
    t = b * n_steps + s
    copies, step = _paged_attention(fox, pt_ref, refs[:n_in], refs[n_in], refs[n_in + 1:],
                                    pages_per_step=pages_per_step, n_steps=n_steps, page=page,
                                    n_new=n_new, lam0=lam0)

    @pl.when(t == 0)
    def _():
        _start_all(copies(b, s, t % 2))

    @pl.when(t + 1 < nb * n_steps)
    def _():
        t1 = t + 1
        _start_all(copies(t1 // n_steps, t1 % n_steps, t1 % 2))

    step(b, s, t % 2)


def _start_all(cps):
    for n, cp in enumerate(cps):
        cp.start(priority=n % 2)


def _paged_attention(fox, pt_ref, ins, o_ref, scratch, *, pages_per_step, n_steps, page, n_new, lam0):
    if fox:
        q_ref, kn_ref, vn_ref, c_ref, kc_ref, vc_ref = ins
    else:
        q_ref, kn_ref, vn_ref, lq1, lk1, lq2, lk2, g_ref, kc_ref, vc_ref = ins
    kbuf, vbuf, sem, qbd_ref, m_ref, l_ref, acc_ref, cq_ref = scratch
    w = q_ref.shape[1]
    n_rows = qbd_ref.shape[0]
    n_groups = n_rows // n_new
    n_heads_v = w // LANES
    n_past = n_steps * pages_per_step * page
    chunk = pages_per_step * page

    def copies(bb, ss, sl):
        out = []
        for p in range(pages_per_step):
            pg = pt_ref[bb, ss * pages_per_step + p]
            out.append(pltpu.make_async_copy(kc_ref.at[pg], kbuf.at[sl, :, pl.ds(p * page, page)],
                                             sem.at[sl, 0]))
            if fox:
                vdst = vbuf.at[sl, :, pl.ds(p * page, page)]
            else:
                vdst = vbuf.at[sl, pl.ds(p * page * n_heads_v, page * n_heads_v), :]
            out.append(pltpu.make_async_copy(vc_ref.at[pg], vdst, sem.at[sl, 1]))
        return out

    row = lax.broadcasted_iota(jnp.int32, (n_rows, 1), 0)

    def bias(col0, width):
        cb = c_ref[0, :, pl.ds(col0, width)]
        return jnp.broadcast_to(cb[:, None, :], (n_groups, n_new, width)).reshape(n_rows, width)

    def init():
        qt = jnp.concatenate([q_ref[...]] * n_groups, axis=0)
        colg = lax.broadcasted_iota(jnp.int32, (1, w), 1) // HEAD_DIM
        qbd_ref[...] = jnp.where(row // n_new == colg, qt, 0.0).astype(BF16)
        m_ref[...] = jnp.full(m_ref.shape, -jnp.inf, F32)
        l_ref[...] = jnp.zeros(l_ref.shape, F32)
        acc_ref[...] = jnp.zeros(acc_ref.shape, F32)
        if fox:
            lane = lax.broadcasted_iota(jnp.int32, (1, page), 1)
            cq_ref[...] = jnp.sum(jnp.where(lane == row % n_new, bias(n_past, page), 0.0),
                                  axis=1, keepdims=True)
        else:
            cq_ref[...] = jnp.zeros(cq_ref.shape, F32)

    def update(sc, pv):
        cq = cq_ref[...]
        m_prev = m_ref[...]
        m_new = jnp.maximum(m_prev, cq + jnp.max(sc, axis=1, keepdims=True))
        p = jnp.exp(sc + (cq - m_new))
        alpha = jnp.exp(m_prev - m_new)
        l_ref[...] = alpha * l_ref[...] + jnp.sum(p, axis=1, keepdims=True)
        acc_ref[...] = alpha * acc_ref[...] + pv(p.astype(BF16))
        m_ref[...] = m_new

    def pv_heads(p, v_of_head):
        rows_per_head = 2 * n_new
        return jnp.concatenate(
            [_nn(p[h * rows_per_head:(h + 1) * rows_per_head], v_of_head(h)) for h in range(n_heads_v)],
            axis=0)

    def finish():
        pad = jnp.zeros((page - n_new, w), F32)
        kn = jnp.concatenate([kn_ref[...], pad], axis=0).astype(BF16)
        vn = jnp.concatenate([vn_ref[...], pad], axis=0).astype(BF16)
        sn = _nt(qbd_ref[...], kn)
        if fox:
            sn = sn - bias(n_past, page)
        lane = lax.broadcasted_iota(jnp.int32, (1, page), 1)
        sn = jnp.where(lane <= row % n_new, sn, NEG_INF)
        if fox:
            update(sn, lambda p: _nn(p, vn))
        else:
            update(sn, lambda p: pv_heads(p, lambda h: vn[:, h * LANES:(h + 1) * LANES]))

        on = acc_ref[...] / l_ref[...]
        if fox:
            col = lax.broadcasted_iota(jnp.int32, (1, 1, w), 2)
            grp = lax.broadcasted_iota(jnp.int32, (n_groups, 1, 1), 0)
            o3 = on.reshape(n_groups, n_new, w)
            o_ref[...] = jnp.sum(jnp.where(grp == col // HEAD_DIM, o3, 0.0), axis=0)
        else:
            lam = _lambda(lq1[...], lk1[...], lq2[...], lk2[...], lam0)
            for h in range(n_heads_v):
                r0 = 2 * h * n_new
                oh = on[r0:r0 + n_new] - lam * on[r0 + n_new:r0 + 2 * n_new]
                o_ref[:, h * LANES:(h + 1) * LANES] = _rms(oh, g_ref[...]) * (1.0 - lam0)

    def step(b, s, slot):
        pl.when(s == 0)(init)
        for cp in copies(b, s, slot):
            cp.wait()
        sc = _nn(qbd_ref[...], kbuf[slot].astype(BF16))
        if fox:
            sc = sc - bias(pl.multiple_of(s * chunk, chunk), chunk)
            vt = vbuf[slot].astype(BF16)
            update(sc, lambda p: _nt(p, vt))
        else:
            update(sc, lambda p: pv_heads(
                p, lambda h: vbuf[slot, pl.ds(h, chunk, stride=n_heads_v), :].astype(BF16)))
        pl.when(s == n_steps - 1)(finish)

    return copies, step


def sample_attention(page_table, q, k_new, v_new, k_cache_t, v_cache, pages_per_step, *,
                     c_all=None, lams=None, gain=None, lam0=0.0):
    fox = c_all is not None
    nb, n_pages = page_table.shape
    _, w, page = k_cache_t.shape
    n_new = q.shape[0] // nb
    n_steps = n_pages // pages_per_step
    chunk = pages_per_step * page
    n_rows = (w // HEAD_DIM) * n_new
    new_spec = pl.BlockSpec((n_new, w), lambda b, s, pt: (b, 0))
    any_spec = pl.BlockSpec(memory_space=pl.ANY)
    if fox:
        ins = [q, k_new, v_new, c_all, k_cache_t, v_cache]
        in_specs = [new_spec, new_spec, new_spec,
                    pl.BlockSpec((1,) + c_all.shape[1:], lambda b, s, pt: (b, 0, 0)),
                    any_spec, any_spec]
        v_scratch = pltpu.VMEM((2, w, chunk), F32)
        acc_cols = w
    else:
        vec = pl.BlockSpec((1, HEAD_DIM), lambda b, s, pt: (0, 0))
        ins = ([q, k_new, v_new] + [x.reshape(1, HEAD_DIM) for x in lams]
               + [gain.reshape(1, LANES), k_cache_t, v_cache])
        in_specs = [new_spec, new_spec, new_spec, vec, vec, vec, vec,
                    pl.BlockSpec((1, LANES), lambda b, s, pt: (0, 0)), any_spec, any_spec]
        v_scratch = pltpu.VMEM((2, chunk * (w // LANES), LANES), F32)
        acc_cols = LANES
    grid_spec = pltpu.PrefetchScalarGridSpec(
        num_scalar_prefetch=1, grid=(nb, n_steps), in_specs=in_specs,
        out_specs=new_spec,
        scratch_shapes=[pltpu.VMEM((2, w, chunk), F32), v_scratch,
                        pltpu.SemaphoreType.DMA((2, 2)),
                        pltpu.VMEM((n_rows, w), BF16),
                        pltpu.VMEM((n_rows, 1), F32), pltpu.VMEM((n_rows, 1), F32),
                        pltpu.VMEM((n_rows, acc_cols), F32), pltpu.VMEM((n_rows, 1), F32)])
    nbytes = 4 * chunk * w * 4 + 2 * chunk * w * 2 + 4 * n_rows * chunk * 4
    if fox:
        nbytes += 2 * c_all.shape[1] * c_all.shape[2] * 4
    return pl.pallas_call(
        functools.partial(_sample_attn_kernel, fox=fox, pages_per_step=pages_per_step, n_steps=n_steps,
                          page=page, n_new=n_new, lam0=lam0),
        grid_spec=grid_spec,
        out_shape=jax.ShapeDtypeStruct((nb * n_new, w), F32),
        compiler_params=_params(2, nbytes),
        name="fox_sample" if fox else "diff_sample",
    )(page_table, *ins)


def _merge_kernel(fo_ref, do_ref, wf_ref, wd_ref, gf_ref, gd_ref, o_ref, wfb, wdb):
    @pl.when(pl.program_id(1) == 0)
    def _():
        wfb[...] = wf_ref[0].astype(BF16)
        wdb[...] = wd_ref[0].astype(BF16)

    pf = _nn(fo_ref[...].astype(BF16), wfb[...])
    pd = _nn(do_ref[...].astype(BF16), wdb[...])
    o_ref[...] = (jax.nn.sigmoid(gf_ref[...]) * pf + jax.nn.sigmoid(gd_ref[...]) * pd).astype(o_ref.dtype)


def branch_merge(fo, do, w_branch, gates, tm, tn):
    t, bw = fo.shape
    d = w_branch.shape[2]
    nn = d // tn
    nbytes = 4 * tm * bw * fo.dtype.itemsize + 4 * bw * tn * 4 + 2 * bw * tn * 2 + 6 * tm * tn * 4
    return pl.pallas_call(
        _merge_kernel, grid=(nn, t // tm),
        in_specs=[pl.BlockSpec((tm, bw), lambda j, i: (i, 0)),
                  pl.BlockSpec((tm, bw), lambda j, i: (i, 0)),
                  pl.BlockSpec((1, bw, tn), lambda j, i: (0, 0, j)),
                  pl.BlockSpec((1, bw, tn), lambda j, i: (1, 0, j)),
                  pl.BlockSpec((tm, tn), lambda j, i: (i, j)),
                  pl.BlockSpec((tm, tn), lambda j, i: (i, nn + j))],
        out_specs=pl.BlockSpec((tm, tn), lambda j, i: (i, j)),
        out_shape=jax.ShapeDtypeStruct((t, d), BF16),
        scratch_shapes=[pltpu.VMEM((bw, tn), BF16), pltpu.VMEM((bw, tn), BF16)],
        compiler_params=_params(2, nbytes),
        name="branch_merge",
    )(fo, do, w_branch, w_branch, gates, gates)


def _gelu_tanh(x):
    return 0.5 * x * (1.0 + jnp.tanh(math.sqrt(2.0 / math.pi) * (x + 0.044715 * (x * x * x))))


def _causal_conv(u, cw, prev1, prev2, period):
    n = u.shape[0]
    pos = lax.broadcasted_iota(jnp.int32, (n, 1), 0) % period
    u1 = jnp.where(pos == 0, prev1, pltpu.roll(u, 1, 0))
    u2 = jnp.where(pos == 0, prev2, jnp.where(pos == 1, prev1, pltpu.roll(u, 2, 0)))
    return cw[0:1] * u2 + cw[1:2] * u1 + cw[2:3] * u


def _ffn_prompt_kernel(h_ref, wg_ref, wv_ref, cwg_ref, cwv_ref, wd_ref, f_ref, cs_ref,
                       carry_ref, *, tiles_per_seq, sub):
    i = pl.program_id(0)
    f = pl.program_id(1)
    tm = h_ref.shape[0]
    tf = wg_ref.shape[1]
    h = h_ref[...]

    @pl.when(i % tiles_per_seq == 0)
    def _():
        carry_ref[f] = jnp.zeros(carry_ref.shape[1:], F32)

    @pl.when(f == 0)
    def _():
        f_ref[...] = jnp.zeros(f_ref.shape, F32)

    acts = []
    for c in range(tf // sub):
        cs = slice(c * sub, (c + 1) * sub)
        ug = _nn(h, wg_ref[:, cs])
        uv = _nn(h, wv_ref[:, cs])
        pg = carry_ref[f, 0, :, cs]
        pv = carry_ref[f, 1, :, cs]
        yg = _causal_conv(ug, cwg_ref[:, cs], pg[SUBLANES - 1:SUBLANES], pg[SUBLANES - 2:SUBLANES - 1], tm)
        yv = _causal_conv(uv, cwv_ref[:, cs], pv[SUBLANES - 1:SUBLANES], pv[SUBLANES - 2:SUBLANES - 1], tm)
        carry_ref[f, 0, :, cs] = ug[tm - SUBLANES:tm]
        carry_ref[f, 1, :, cs] = uv[tm - SUBLANES:tm]
        cs_ref[0, 0, :, cs] = ug[tm - (CONV_WIDTH - 1):tm]
        cs_ref[0, 1, :, cs] = uv[tm - (CONV_WIDTH - 1):tm]
        acts.append((_gelu_tanh(yg) * yv).astype(BF16))

    f_ref[...] += _nn(jnp.concatenate(acts, axis=1), wd_ref[...])


def cast_bf16(w):
    n, m = w.shape
    rows = _tile(n, max(SUBLANES, (4 * 1024 * 1024) // (4 * m)))

    def kern(w_ref, o_ref):
        o_ref[...] = w_ref[...].astype(BF16)

    return pl.pallas_call(
        kern, grid=(n // rows,),
        in_specs=[pl.BlockSpec((rows, m), lambda i: (i, 0))],
        out_specs=pl.BlockSpec((rows, m), lambda i: (i, 0)),
        out_shape=jax.ShapeDtypeStruct((n, m), BF16),
        compiler_params=_params(1, 2 * rows * m * 6),
        name="cast_bf16",
    )(w)


def ffn_prompt(h, w_up, conv_w, w_down, seq, tm, tf, sub):
    t, d = h.shape
    dff = w_down.shape[0]
    nf = dff // tf
    tiles_per_seq = seq // tm
    nbytes = (2 * tm * d * 2 + 4 * d * tf * 2 + 2 * tf * d * 2 + 2 * tm * d * 4
              + 10 * tm * tf * 4 + nf * 2 * SUBLANES * tf * 4)
    return pl.pallas_call(
        functools.partial(_ffn_prompt_kernel, tiles_per_seq=tiles_per_seq, sub=sub),
        grid=(t // tm, nf),
        in_specs=[pl.BlockSpec((tm, d), lambda i, f: (i, 0)),
                  pl.BlockSpec((d, tf), lambda i, f: (0, f)),
                  pl.BlockSpec((d, tf), lambda i, f: (0, nf + f)),
                  pl.BlockSpec((CONV_WIDTH, tf), lambda i, f: (0, f)),
                  pl.BlockSpec((CONV_WIDTH, tf), lambda i, f: (0, nf + f)),
                  pl.BlockSpec((tf, d), lambda i, f: (f, 0))],
        out_specs=[pl.BlockSpec((tm, d), lambda i, f: (i, 0)),
                   pl.BlockSpec((1, 2, CONV_WIDTH - 1, tf), lambda i, f: (i, 0, 0, f))],
        out_shape=[jax.ShapeDtypeStruct((t, d), F32),
                   jax.ShapeDtypeStruct((t // tm, 2, CONV_WIDTH - 1, dff), F32)],
        scratch_shapes=[pltpu.VMEM((nf, 2, SUBLANES, tf), F32)],
        compiler_params=_params(2, nbytes),
        name="ffn_prompt",
    )(h, w_up, w_up, conv_w, conv_w, w_down)


def _ffn_sample_kernel(h_ref, wg_ref, wv_ref, cwg_ref, cwv_ref, wd_ref, sg_ref, sv_ref, f_ref, cs_ref,
                       *, n_new):
    f = pl.program_id(0)
    t = h_ref.shape[0]
    nb = t // n_new
    tf = wg_ref.shape[1]
    h = h_ref[...]
    ug = _nn(h, wg_ref[...])
    uv = _nn(h, wv_ref[...])

    def rows_of(state, r):
        return jnp.broadcast_to(state[:, r:r + 1, :], (nb, n_new, tf)).reshape(t, tf)

    sg = sg_ref[...]
    sv = sv_ref[...]
    yg = _causal_conv(ug, cwg_ref[...], rows_of(sg, 1), rows_of(sg, 0), n_new)
    yv = _causal_conv(uv, cwv_ref[...], rows_of(sv, 1), rows_of(sv, 0), n_new)
    keep = CONV_WIDTH - 1
    cs_ref[0] = ug.reshape(nb, n_new, tf)[:, n_new - keep:, :]
    cs_ref[1] = uv.reshape(nb, n_new, tf)[:, n_new - keep:, :]

    act = (_gelu_tanh(yg) * yv).astype(BF16)

    @pl.when(f == 0)
    def _():
        f_ref[...] = jnp.zeros(f_ref.shape, F32)

    f_ref[...] += _nn(act, wd_ref[...])


def ffn_sample(h, w_up, conv_w, w_down, state, n_new, tf):
    t, d = h.shape
    dff = w_down.shape[0]
    nf = dff // tf
    nb = t // n_new
    keep = CONV_WIDTH - 1
    nbytes = (2 * t * d * 2 + 4 * d * tf * 2 + 2 * tf * d * 2 + 2 * t * d * 4
              + 12 * t * tf * 4 + 8 * nb * SUBLANES * tf * 4)
    return pl.pallas_call(
        functools.partial(_ffn_sample_kernel, n_new=n_new),
        grid=(nf,),
        in_specs=[pl.BlockSpec((t, d), lambda f: (0, 0)),
                  pl.BlockSpec((d, tf), lambda f: (0, f)),
                  pl.BlockSpec((d, tf), lambda f: (0, nf + f)),
                  pl.BlockSpec((CONV_WIDTH, tf), lambda f: (0, f)),
                  pl.BlockSpec((CONV_WIDTH, tf), lambda f: (0, nf + f)),
                  pl.BlockSpec((tf, d), lambda f: (f, 0)),
                  pl.BlockSpec((nb, keep, tf), lambda f: (0, 0, f)),
                  pl.BlockSpec((nb, keep, tf), lambda f: (0, 0, nf + f))],
        out_specs=[pl.BlockSpec((t, d), lambda f: (0, 0)),
                   pl.BlockSpec((2, nb, keep, tf), lambda f: (0, 0, 0, f))],
        out_shape=[jax.ShapeDtypeStruct((t, d), F32),
                   jax.ShapeDtypeStruct((2, nb, keep, dff), F32)],
        compiler_params=_params(1, nbytes),
        name="ffn_sample",
    )(h, w_up, w_up, conv_w, conv_w, w_down, state, state)


def _lambda_init(layer_idx):
    return 0.8 - 0.6 * math.exp(-0.3 * layer_idx)


def _rope_angles(pos):
    half = HEAD_DIM // 2
    inv_freq = ROPE_THETA ** (-jnp.arange(half, dtype=F32) / half)
    ang = pos.astype(F32)[:, None] * inv_freq[None, :]
    return jnp.cos(ang), jnp.sin(ang)


def _rope_tables(pos):
    cos, sin = _rope_angles(pos)
    return (jnp.concatenate([cos, cos, cos, cos], axis=1),
            jnp.concatenate([-sin, sin, -sin, sin], axis=1))


def _out_proj_kernel(a_ref, w_ref, x_ref, g1_ref, g2_ref, o_ref, h_ref):
    mix = _nn(a_ref[...], w_ref[...])
    y = x_ref[...] + _rms(mix, g1_ref[...])
    o_ref[...] = y
    h_ref[...] = _rms(y, g2_ref[...]).astype(h_ref.dtype)


def out_proj_norm(merged, w_out_b, x, g1, g2, tm):
    t, d = x.shape
    row = pl.BlockSpec((tm, d), lambda i: (i, 0))
    vec = pl.BlockSpec((1, d), lambda i: (0, 0))
    return pl.pallas_call(
        _out_proj_kernel, grid=(t // tm,),
        in_specs=[row, pl.BlockSpec((d, d), lambda i: (0, 0)), row, vec, vec],
        out_specs=[row, row],
        out_shape=[jax.ShapeDtypeStruct((t, d), F32), jax.ShapeDtypeStruct((t, d), BF16)],
        compiler_params=_params(1, 2 * d * d * 2 + 2 * tm * d * (2 + 4 + 4 + 2) + 2 * tm * d * 4),
        name="out_proj_norm",
    )(merged, w_out_b, x, g1.reshape(1, d), g2.reshape(1, d))


def _mixer_tail(x, fo, do, gates, w_branch_l, w_out_b, g_post, g_ffn_pre, tm, tn):
    merged = branch_merge(fo, do, w_branch_l, gates, tm, tn)
    return out_proj_norm(merged, w_out_b, x, g_post, g_ffn_pre, _tile(x.shape[0], 512))


def kernel(x_prompt, x_sample, cache_fox_k, cache_fox_v, cache_fox_logf, cache_diff_k, cache_diff_v,
           state_ffn_conv, page_table, norm_mix_pre, norm_mix_post, w_in, b_forget, lam_q1, lam_k1,
           lam_q2, lam_k2, diff_subln, w_branch, w_out, norm_ffn_pre, norm_ffn_post, w_up, conv_ffn,
           w_down):
    nbp, seq, d = x_prompt.shape
    nbs, n_new, _ = x_sample.shape
    depth, n_pool, page, fox_heads, _ = cache_fox_k.shape
    diff_heads = cache_diff_v.shape[3]
    fw = fox_heads * HEAD_DIM
    dqk = 2 * diff_heads * HEAD_DIM
    dvw = diff_heads * 2 * HEAD_DIM
    dff = w_down.shape[1]
    n_pages = page_table.shape[1]
    n_past = n_pages * page
    tp, ts = nbp * seq, nbs * n_new

    tm = _tile(seq, ROW_TILE)
    tq = _tile(seq, ATTN_TILE)
    tf = _tile(dff, FF_TILE)
    tn = _tile(fw, COL_TILE)
    tn_d = _tile(d, COL_TILE)
    pages_per_step = _tile(n_pages, PAGES_PER_STEP)
    assert dqk == fw and dvw == fw and fw % tn == 0 and (2 * d) % tn == 0

    tps = seq // tm
    cos_p, sin_p = _rope_tables(jnp.arange(seq))
    cos_pt, sin_pt = [x.T for x in _rope_angles(jnp.arange(seq))]
    cos_s, sin_s = _rope_tables(n_past + jnp.tile(jnp.arange(n_new), nbs))

    def rope_extras(cos, sin, rows, period):
        spec = pl.BlockSpec((rows, LANES), lambda j, i: (i % period, 0))
        return [(cos, spec), (sin, spec)]

    xp = x_prompt.reshape(tp, d)
    xs = x_sample.reshape(ts, d)
    outs_p = [[] for _ in range(6)]
    outs_s = [[] for _ in range(6)]
    nblk = fw // tn
    for l in range(depth):
        lam0 = _lambda_init(l)
        lams = (lam_q1[l], lam_k1[l], lam_q2[l], lam_k2[l])
        w_t = w_in[l].T
        w_ff_t = w_t[3 * fw:3 * fw + fox_heads]
        rest_row0 = 3 * fw + fox_heads
        proj = functools.partial(matmul, w_transposed=True, n_blocks=nblk, tn=tn)

        hp, logf = norm_forget_gate(xp, norm_mix_pre[l], w_ff_t, b_forget[l], seq, tm)
        pj = functools.partial(proj, hp, tm=tm)
        fm_pair = [_feature_out(nbp, fw, seq, tm, tn, F32), _feature_out(nbp, fw, seq, tm, tn, BF16)]
        fq = pj(w_t, col_block=0, outs=[_token_out(tp, fw, tm, tn, BF16)],
                epilogue=_epi_plain(Q_SCALE * LOG2E))[0]
        fk, fkb = pj(w_t, col_block=nblk, outs=fm_pair, epilogue=_epi_plain(1.0, True), feature_major=True)
        fv, fvb = pj(w_t, col_block=2 * nblk, outs=fm_pair, epilogue=_epi_plain(1.0, True),
                     feature_major=True)
        dq = pj(w_t, row_offset=rest_row0,col_block=0, outs=[_token_out(tp, dqk, tm, tn, BF16)],
                epilogue=_epi_rope(Q_SCALE * LOG2E), extras=rope_extras(cos_p, sin_p, tm, tps))[0]
        fm_spec = pl.BlockSpec((HEAD_DIM // 2, tm), lambda j, i: (0, i % tps))
        dk, dkb = pj(w_t, row_offset=rest_row0,col_block=nblk, outs=fm_pair, epilogue=_epi_rope_fm,
                     extras=[(cos_pt, fm_spec), (sin_pt, fm_spec)], feature_major=True)
        dv, dvb = pj(w_t, row_offset=rest_row0,col_block=2 * nblk,
                     outs=[(jax.ShapeDtypeStruct((tp, dvw // LANES, LANES), F32),
                            pl.BlockSpec((tm, tn // LANES, LANES), lambda j, i: (i, j, 0))),
                           _token_out(tp, dvw, tm, tn, BF16)],
                     epilogue=_epi_heads)
        gates = matmul(hp, w_t, row_offset=rest_row0, w_transposed=True, col_block=3 * nblk, n_blocks=2 * d // tn, tn=tn,
                       tm=tm, outs=[_token_out(tp, 2 * d, tm, tn, F32)], epilogue=_epi_plain(1.0))[0]
        c_t, c = cumsum_prompt(logf, tq)
        fo = fox_prompt(fq, fkb, fvb, c, c_t, tq)
        do = diff_prompt(dq, dkb, dvb, lams, diff_subln[l], lam0, tq)
        w_out_b = cast_bf16(w_out[l])
        x2, h2 = _mixer_tail(xp, fo, do, gates, w_branch[l], w_out_b, norm_mix_post[l], norm_ffn_pre[l],
                             tm, tn_d)
        w_up_b = cast_bf16(w_up[l])
        w_down_b = cast_bf16(w_down[l])
        f, conv_p = ffn_prompt(h2, w_up_b, conv_ffn[l], w_down_b, seq, tm, tf, _tile(tf, FF_SUB))
        xp = residual_norm(x2, f, norm_ffn_post[l], _tile(tp, 512))
        conv_p = conv_p[tps - 1::tps]
        conv_p = jnp.transpose(conv_p, (0, 2, 1, 3)).reshape(nbp, CONV_WIDTH - 1, 2 * dff)

        def heads_last(x_t, n_heads):
            return jnp.transpose(x_t.reshape(nbp, n_heads, HEAD_DIM, seq), (0, 3, 1, 2))

        for lst, val in zip(outs_p, (heads_last(fk, fox_heads), heads_last(fv, fox_heads),
                                     jnp.transpose(logf, (0, 2, 1)),
                                     heads_last(dk, 2 * diff_heads),
                                     dv.reshape(nbp, seq, diff_heads, 2 * HEAD_DIM), conv_p)):
            lst.append(val)

        hs, logf = norm_forget_gate(xs, norm_mix_pre[l], w_ff_t, b_forget[l], ts, ts)
        logf = logf[0]
        pj = functools.partial(proj, hs, tm=ts)
        tok = lambda n: [_token_out(ts, n, ts, tn, F32)]
        fq = pj(w_t, col_block=0, outs=tok(fw), epilogue=_epi_plain(Q_SCALE))[0]
        fk = pj(w_t, col_block=nblk, outs=tok(fw), epilogue=_epi_plain(1.0))[0]
        fv = pj(w_t, col_block=2 * nblk, outs=tok(fw), epilogue=_epi_plain(1.0))[0]
        dq = pj(w_t, row_offset=rest_row0,col_block=0, outs=tok(dqk), epilogue=_epi_rope(Q_SCALE),
                extras=rope_extras(cos_s, sin_s, ts, 1))[0]
        dk = pj(w_t, row_offset=rest_row0,col_block=nblk, outs=tok(dqk), epilogue=_epi_rope(1.0),
                extras=rope_extras(cos_s, sin_s, ts, 1))[0]
        dv = pj(w_t, row_offset=rest_row0,col_block=2 * nblk, outs=tok(dvw), epilogue=_epi_plain(1.0))[0]
        gates = matmul(hs, w_t, row_offset=rest_row0, w_transposed=True, col_block=3 * nblk, n_blocks=2 * d // tn, tn=tn,
                       tm=ts, outs=[_token_out(ts, 2 * d, ts, tn, F32)], epilogue=_epi_plain(1.0))[0]
        feature_major = lambda cache: jnp.transpose(cache, (0, 2, 3, 1)).reshape(n_pool, -1, page)
        c_all = cumsum_sample(page_table, logf, jnp.transpose(cache_fox_logf[l], (0, 2, 1)), n_new)
        fo = sample_attention(page_table, fq, fk, fv, feature_major(cache_fox_k[l]),
                              feature_major(cache_fox_v[l]), pages_per_step, c_all=c_all)
        do = sample_attention(page_table, dq, dk, dv, feature_major(cache_diff_k[l]),
                              cache_diff_v[l].reshape(n_pool, page * diff_heads, 2 * HEAD_DIM),
                              pages_per_step, lams=lams, gain=diff_subln[l], lam0=lam0)
        x2, h2 = _mixer_tail(xs, fo, do, gates, w_branch[l], w_out_b, norm_mix_post[l], norm_ffn_pre[l],
                             ts, tn_d)
        f, conv_s = ffn_sample(h2, w_up_b, conv_ffn[l], w_down_b, state_ffn_conv[l], n_new, tf)
        xs = residual_norm(x2, f, norm_ffn_post[l], ts)
        conv_s = jnp.transpose(conv_s, (1, 2, 0, 3)).reshape(nbs, CONV_WIDTH - 1, 2 * dff)
        for lst, val in zip(outs_s, (fk.reshape(nbs, n_new, fox_heads, HEAD_DIM),
                                     fv.reshape(nbs, n_new, fox_heads, HEAD_DIM),
                                     logf.T.reshape(nbs, n_new, fox_heads),
                                     dk.reshape(nbs, n_new, 2 * diff_heads, HEAD_DIM),
                                     dv.reshape(nbs, n_new, diff_heads, 2 * HEAD_DIM), conv_s)):
            lst.append(val)

    return (xp.reshape(nbp, seq, d), xs.reshape(nbs, n_new, d),
            *[jnp.stack(v) for v in outs_p], *[jnp.stack(v) for v in outs_s])
```
